```python
import functools
import jax, jax.numpy as jnp
from jax import lax
import numpy as np

D_MODEL = 1024
BATCH = 32
SEQ = 256
DEPTH = 4
DEC_BATCH = 2
DEC_SEQ = 2048
PAST_LEN = 512

GRID_W = 64
N_HEADS = 8
HEAD_DIM = 64
ATTN_W = N_HEADS * HEAD_DIM
NA_KH = 8
NA_KW = 16
POOL_WINDOWS = (2, 4, 8, 16)
N_POOL = 4
POOL_GRP = 128
POOL_W = N_POOL * POOL_GRP
LRU_W = 512
LRU_BLOCKS = 8
LRU_BLK = LRU_W // LRU_BLOCKS
LRU_C = 8.0
CONV_W = 4
N_BRANCH = 3
FF_HIDDEN = -(-8 * D_MODEL // (3 * 256)) * 256
SPLITS = (POOL_W, POOL_W + ATTN_W, POOL_W + 2 * ATTN_W, POOL_W + 3 * ATTN_W,
          POOL_W + 3 * ATTN_W + LRU_W, POOL_W + 3 * ATTN_W + 2 * LRU_W)
IN_W = POOL_W + 3 * ATTN_W + 2 * LRU_W + N_BRANCH * D_MODEL
Q_BLOCK = 128
EPS = 1e-6
NEG_INF = -1e30
ATTN_SCALE = HEAD_DIM ** -0.5

kernel_name = "hybrid_flow_pool_na_rglru_step"


def rmsnorm(x, g):
    xf = x.astype(jnp.float32)
    y = xf * lax.rsqrt(jnp.mean(xf * xf, axis=-1, keepdims=True) + EPS)
    return (y * g.astype(jnp.float32)).astype(x.dtype)


def pool_branch(up, pool_w, pool_scale):
    b, l, _ = up.shape
    xg = up.astype(jnp.float32).reshape(b, l, N_POOL, POOL_GRP)
    s = jnp.concatenate([jnp.zeros((b, 1, N_POOL, POOL_GRP), jnp.float32),
                         jnp.cumsum(xg, axis=1)], axis=1)
    t = jnp.arange(l)[:, None]
    half = jnp.array(POOL_WINDOWS, jnp.int32)[None, :] // 2
    lo = jnp.clip(t - half, 0, l)
    hi = jnp.clip(t + half, 0, l)
    g = jnp.arange(N_POOL)[None, :]
    cnt = (hi - lo).astype(jnp.float32)[None, :, :, None]
    d = (s[:, hi, g] - s[:, lo, g]) / cnt - xg
    y = jnp.einsum('blgc,gcd->blgd', d, pool_w.astype(jnp.float32)).reshape(b, l, POOL_W)
    return (y * pool_scale.astype(jnp.float32)).astype(up.dtype)


def context_attention(q, k, v):
    b, l, h, d = q.shape
    nb = l // Q_BLOCK
    qb = q.reshape(b, nb, Q_BLOCK, h, d).transpose(1, 0, 2, 3, 4)

    def one_block(qi):
        s = jnp.einsum('bqhd,bkhd->bhqk', qi, k).astype(jnp.float32) * ATTN_SCALE
        p = jax.nn.softmax(s, axis=-1).astype(v.dtype)
        return jnp.einsum('bhqk,bkhd->bqhd', p, v)

    o = lax.map(one_block, qb)
    return o.transpose(1, 0, 2, 3, 4).reshape(b, l, h, d)


def neighborhood_attention(q, k, v, ck, cv, rpb):
    b, l, h, d = q.shape
    rows = l // GRID_W
    w = GRID_W
    kh = min(NA_KH, rows)
    qg = q.reshape(b, rows, w, h, d)
    kg = k.reshape(b, rows, w, h, d)
    vg = v.reshape(b, rows, w, h, d)
    r = jnp.arange(rows)
    row0 = jnp.clip(r - kh // 2, 0, rows - kh)
    ridx = row0[:, None] + jnp.arange(kh)[None, :]
    kb = kg[:, ridx].reshape(b, rows, kh * w, h, d)
    vb = vg[:, ridx].reshape(b, rows, kh * w, h, d)
    cidx = jnp.arange(w)
    col0 = jnp.clip(cidx - NA_KW // 2, 0, w - NA_KW)
    col_ok = (cidx[None, :] >= col0[:, None]) & (cidx[None, :] < col0[:, None] + NA_KW)
    mask = jnp.broadcast_to(col_ok[:, None, :], (w, kh, w)).reshape(w, kh * w)
    dr = ridx - r[:, None] + (NA_KH - 1)
    dc = jnp.clip(cidx[None, :] - cidx[:, None], -(NA_KW - 1), NA_KW - 1) + (NA_KW - 1)
    bias = rpb[:, dr[:, None, :, None], dc[None, :, None, :]]
    bias = bias.reshape(h, rows, w, kh * w).astype(jnp.float32)
    s_lat = jnp.einsum('brqhd,brkhd->bhrqk', qg, kb).astype(jnp.float32) * ATTN_SCALE + bias
    s_lat = jnp.where(mask, s_lat, NEG_INF)
    s_ctx = jnp.einsum('brqhd,bkhd->bhrqk', qg, ck).astype(jnp.float32) * ATTN_SCALE
    p = jax.nn.softmax(jnp.concatenate([s_lat, s_ctx], axis=-1), axis=-1).astype(v.dtype)
    p_lat, p_ctx = p[..., :kh * w], p[..., kh * w:]
    o = (jnp.einsum('bhrqk,brkhd->brqhd', p_lat, vb)
         + jnp.einsum('bhrqk,bkhd->brqhd', p_ctx, cv))
    return o.reshape(b, l, h, d)


def linear_scan(a, bx, h0, reverse):
    def comb(left, right):
        return (left[0] * right[0], right[0] * left[1] + right[1])
    a_cum, b_cum = lax.associative_scan(comb, (a, bx), axis=1, reverse=reverse)
    return a_cum * h0[:, None, :] + b_cum


def rglru_branch(ux, conv_w, conv_b, wa, ba, wx, bx, lam, h0f, h0b):
    b, l, _ = ux.shape
    lpad = CONV_W // 2
    xp = jnp.pad(ux, ((0, 0), (lpad, CONV_W - 1 - lpad), (0, 0)))
    xc = conv_b
    for j in range(CONV_W):
        xc = xc + conv_w[j] * xp[:, j:j + l]
    xf = xc.astype(jnp.float32)
    xb = xf.reshape(b, l, LRU_BLOCKS, LRU_BLK)
    rg = jax.nn.sigmoid(jnp.einsum('blnc,encd->eblnd', xb, wa.astype(jnp.float32)).reshape(2, b, l, LRU_W)
                        + ba.astype(jnp.float32)[:, None, None, :])
    ig = jax.nn.sigmoid(jnp.einsum('blnc,encd->eblnd', xb, wx.astype(jnp.float32)).reshape(2, b, l, LRU_W)
                        + bx.astype(jnp.float32)[:, None, None, :])
    log_a = LRU_C * rg * jax.nn.log_sigmoid(lam.astype(jnp.float32))[:, None, None, :]
    a = jnp.exp(log_a)
    inp = jnp.sqrt(-jnp.expm1(2.0 * log_a)) * ig * xf[None]
    hf = linear_scan(a[0], inp[0], h0f, False)
    hb = linear_scan(a[1], inp[1], h0b, True)
    return hf, hb


def block(x, mod, attn_fn, h0f, h0b, g1, g2, w_in, pool_w, pool_scale, conv_w, conv_b,
          wa, ba, wx, bx, lam, w_branch, w_out, w_ff_in, w_ff_out):
    b, l, _ = x.shape
    sh1, sc1, gt1, sh2, sc2, gt2 = jnp.split(mod, 6, axis=-1)
    h = rmsnorm(x, g1) * (1 + sc1) + sh1
    u = h @ w_in
    u_pool, u_q, u_k, u_v, u_x, u_y, u_g = jnp.split(u, SPLITS, axis=-1)
    q = u_q.reshape(b, l, N_HEADS, HEAD_DIM)
    k = u_k.reshape(b, l, N_HEADS, HEAD_DIM)
    v = u_v.reshape(b, l, N_HEADS, HEAD_DIM)
    y_a = pool_branch(u_pool, pool_w, pool_scale)
    y_b = attn_fn(q, k, v).reshape(b, l, ATTN_W)
    hf, hb = rglru_branch(u_x, conv_w, conv_b, wa, ba, wx, bx, lam, h0f, h0b)
    y_c = ((hf + hb) * jax.nn.gelu(u_y.astype(jnp.float32))).astype(x.dtype)
    br = jnp.stack([y_a, y_b.astype(x.dtype), y_c], axis=2)
    proj = jnp.einsum('blnw,nwd->blnd', br, w_branch)
    gates = jax.nn.sigmoid(u_g.astype(jnp.float32)).reshape(b, l, N_BRANCH, D_MODEL)
    merged = jnp.sum(gates * proj.astype(jnp.float32), axis=2).astype(x.dtype)
    x = x + gt1 * (merged @ w_out)
    h2 = rmsnorm(x, g2) * (1 + sc2) + sh2
    fg, fu = jnp.split(h2 @ w_ff_in, 2, axis=-1)
    x = x + gt2 * ((jax.nn.silu(fg) * fu) @ w_ff_out)
    return x, k, v, hf, hb


def setup_inputs(seed: int = 0) -> dict:
    key = jax.random.key(seed)
    ks = jax.random.split(key, 32)
    f32 = jnp.float32
    nrm = lambda k, shape, s: jax.random.normal(k, shape, f32) * s
    u = jax.random.uniform(ks[21], (DEPTH, 2, LRU_W), f32, minval=0.9, maxval=0.999)
    p = u ** (1.0 / LRU_C)
    lam = jnp.log(p) - jnp.log1p(-p)
    return {
        "x_prompt": nrm(ks[0], (BATCH, SEQ, D_MODEL), 1.0),
        "x_sample": nrm(ks[1], (DEC_BATCH, DEC_SEQ, D_MODEL), 1.0),
        "cache_k": nrm(ks[2], (DEC_BATCH, DEPTH, PAST_LEN, N_HEADS, HEAD_DIM), 1.0),
        "cache_v": nrm(ks[3], (DEC_BATCH, DEPTH, PAST_LEN, N_HEADS, HEAD_DIM), 1.0),
        "state_lru": nrm(ks[4], (DEC_BATCH, DEPTH, 2, LRU_W), 0.5),
        "c": nrm(ks[5], (DEC_BATCH, D_MODEL), 1.0),
        "c_ctx": nrm(ks[6], (D_MODEL,), 1.0),
        "w_mod": nrm(ks[7], (DEPTH, D_MODEL, 6 * D_MODEL), 0.5 * D_MODEL ** -0.5),
        "b_mod": nrm(ks[8], (DEPTH, 6 * D_MODEL), 0.01),
        "g_norm1": 1.0 + nrm(ks[9], (DEPTH, D_MODEL), 0.05),
        "g_norm2": 1.0 + nrm(ks[10], (DEPTH, D_MODEL), 0.05),
        "w_in": nrm(ks[11], (DEPTH, D_MODEL, IN_W), D_MODEL ** -0.5),
        "pool_w": nrm(ks[12], (DEPTH, N_POOL, POOL_GRP, POOL_GRP), POOL_GRP ** -0.5),
        "pool_scale": 1.0 + nrm(ks[13], (DEPTH, POOL_W), 0.05),
        "na_rpb": nrm(ks[14], (DEPTH, N_HEADS, 2 * NA_KH - 1, 2 * NA_KW - 1), 0.1),
        "lru_conv_w": nrm(ks[15], (DEPTH, CONV_W, LRU_W), CONV_W ** -0.5),
        "lru_conv_b": nrm(ks[16], (DEPTH, LRU_W), 0.01),
        "lru_wa": nrm(ks[17], (DEPTH, 2, LRU_BLOCKS, LRU_BLK, LRU_BLK), LRU_BLK ** -0.5),
        "lru_ba": nrm(ks[18], (DEPTH, 2, LRU_W), 0.01),
        "lru_wx": nrm(ks[19], (DEPTH, 2, LRU_BLOCKS, LRU_BLK, LRU_BLK), LRU_BLK ** -0.5),
        "lru_bx": nrm(ks[20], (DEPTH, 2, LRU_W), 0.01),
        "lru_lambda": lam,
        "w_branch": nrm(ks[22], (DEPTH, N_BRANCH, ATTN_W, D_MODEL), ATTN_W ** -0.5),
        "w_out": nrm(ks[23], (DEPTH, D_MODEL, D_MODEL), D_MODEL ** -0.5),
        "w_ff_in": nrm(ks[24], (DEPTH, D_MODEL, 2 * FF_HIDDEN), D_MODEL ** -0.5),
        "w_ff_out": nrm(ks[25], (DEPTH, FF_HIDDEN, D_MODEL), FF_HIDDEN ** -0.5),
        "g_final": 1.0 + nrm(ks[26], (D_MODEL,), 0.05),
    }


def reference(x_prompt, x_sample, cache_k, cache_v, state_lru, c, c_ctx, w_mod, b_mod,
              g_norm1, g_norm2, w_in, pool_w, pool_scale, na_rpb, lru_conv_w, lru_conv_b,
              lru_wa, lru_ba, lru_wx, lru_bx, lru_lambda, w_branch, w_out, w_ff_in, w_ff_out,
              g_final):
    lws = [(g_norm1[l], g_norm2[l], w_in[l], pool_w[l], pool_scale[l], lru_conv_w[l], lru_conv_b[l],
            lru_wa[l], lru_ba[l], lru_wx[l], lru_bx[l], lru_lambda[l], w_branch[l], w_out[l],
            w_ff_in[l], w_ff_out[l]) for l in range(DEPTH)]

    xc = x_prompt
    n_ctx = x_prompt.shape[0]
    h0 = jnp.zeros((n_ctx, LRU_W), jnp.float32)
    ks_out, vs_out, hs_out = [], [], []
    for l in range(DEPTH):
        mod_ctx = jax.nn.silu(c_ctx) @ w_mod[l] + b_mod[l]
        xc, k, v, hf, hb = block(xc, mod_ctx, context_attention, h0, h0, *lws[l])
        ks_out.append(k)
        vs_out.append(v)
        hs_out.append(jnp.stack([hf[:, -1], hb[:, 0]], axis=1).astype(x_prompt.dtype))

    xs = x_sample
    for l in range(DEPTH):
        mod_lat = (jax.nn.silu(c) @ w_mod[l] + b_mod[l])[:, None, :]
        attn = functools.partial(neighborhood_attention, ck=cache_k[:, l], cv=cache_v[:, l], rpb=na_rpb[l])
        xs, _, _, _, _ = block(xs, mod_lat, attn,
                               state_lru[:, l, 0].astype(jnp.float32),
                               state_lru[:, l, 1].astype(jnp.float32), *lws[l])

    y_prompt = rmsnorm(xc, g_final)
    y_sample = rmsnorm(xs, g_final)
    new_cache_k = jnp.stack(ks_out, axis=1)
    new_cache_v = jnp.stack(vs_out, axis=1)
    new_state_lru = jnp.stack(hs_out, axis=1)
    return (y_prompt, y_sample, new_cache_k, new_cache_v, new_state_lru)
```

```python
import functools

import jax
import jax.numpy as jnp
from jax import lax
from jax.experimental import pallas as pl
from jax.experimental.pallas import tpu as pltpu

D_MODEL = 1024
DEPTH = 4
GRID_W = 64
N_HEADS = 8
HEAD_DIM = 64
ATTN_W = N_HEADS * HEAD_DIM
NA_KH = 8
NA_KW = 16
POOL_WINDOWS = (2, 4, 8, 16)
N_POOL = 4
POOL_GRP = 128
POOL_W = N_POOL * POOL_GRP
LRU_W = 512
LRU_BLOCKS = 8
LRU_BLK = LRU_W // LRU_BLOCKS
LRU_C = 8.0
CONV_W = 4
N_BRANCH = 3
FF_HIDDEN = 2816
IN_W = POOL_W + 3 * ATTN_W + 2 * LRU_W + N_BRANCH * D_MODEL
EPS = 1e-6
NEG_INF = -1e30
ATTN_SCALE = HEAD_DIM ** -0.5

BF16 = jnp.bfloat16
F32 = jnp.float32

LANES = 128
SUBLANES = 8
MXU_DIM = 256
MIB = 1024 * 1024

ROW_TILE = 256
FF_CHUNK = 1408
MOD_COLS = 1536
POOL_PAD = 16
CONV_PAD = 8
SCAN_CHUNKS = SUBLANES
LAT_ROWS_PER_STEP = 4


def _params(semantics, vmem_mib):
    return pltpu.CompilerParams(dimension_semantics=semantics,
                                vmem_limit_bytes=vmem_mib * MIB)


def _resident(block_shape, index_map):
    return pl.BlockSpec(block_shape, index_map, pipeline_mode=pl.Buffered(1))


def _mm(a, b):
    return jnp.dot(a, b, preferred_element_type=F32)


def _mm_nt(a, b):
    return lax.dot_general(a, b, (((1,), (1,)), ((), ())), preferred_element_type=F32)


def _rmsnorm(x, g):
    ms = jnp.mean(x * x, axis=-1, keepdims=True)
    return (x * lax.rsqrt(ms + EPS)) * g


def _mod_kernel(c_ref, w_ref, b_ref, o_ref):
    c = c_ref[...]
    s = c * jax.nn.sigmoid(c)
    o_ref[...] = jnp.dot(s, w_ref[...], preferred_element_type=F32,
                         precision=lax.Precision.HIGHEST) + b_ref[...]


def _modulation(cond, w_mod, b_mod):
    n = 6 * D_MODEL
    return pl.pallas_call(
        _mod_kernel,
        grid=(DEPTH, n // MOD_COLS),
        in_specs=[
            pl.BlockSpec((SUBLANES, D_MODEL), lambda l, j: (0, 0)),
            pl.BlockSpec((None, D_MODEL, MOD_COLS), lambda l, j: (l, 0, j)),
            pl.BlockSpec((None, 1, MOD_COLS), lambda l, j: (l, 0, j)),
        ],
        out_specs=pl.BlockSpec((None, SUBLANES, MOD_COLS), lambda l, j: (l, 0, j)),
        out_shape=jax.ShapeDtypeStruct((DEPTH, SUBLANES, n), F32),
        compiler_params=_params(("arbitrary", "arbitrary"), 32),
        name="modulation",
    )(cond, w_mod, b_mod.reshape(DEPTH, 1, n))


def _in_proj_kernel(x_ref, mod_ref, g_ref, w_ref, *out_refs):
    up_ref, q_ref, k_ref, v_ref, ux_ref, uy_ref, ug_ref = out_refs[-7:]
    x = x_ref[...]
    sh1 = mod_ref[:, 0:D_MODEL]
    sc1 = mod_ref[:, D_MODEL:2 * D_MODEL]
    h = (_rmsnorm(x, g_ref[...]) * (1.0 + sc1) + sh1).astype(BF16)

    def seg(lo, width):
        return _mm(h, w_ref[:, lo:lo + width])

    off = 0
    up_ref[...] = seg(off, POOL_W)
    off += POOL_W
    q_ref[...] = (seg(off, ATTN_W) * ATTN_SCALE).astype(q_ref.dtype)
    off += ATTN_W
    k_ref[...] = seg(off, ATTN_W).astype(k_ref.dtype)
    off += ATTN_W
    v_ref[...] = seg(off, ATTN_W).astype(v_ref.dtype)
    off += ATTN_W
    ux_ref[...] = seg(off, LRU_W)
    off += LRU_W
    uy_ref[...] = seg(off, LRU_W)
    off += LRU_W
    for n in range(N_BRANCH):
        ug_ref[:, n * D_MODEL:(n + 1) * D_MODEL] = seg(off + n * D_MODEL, D_MODEL)


def _in_proj(x, mod, g1, w_in, layer, *, rows_per_cond, cond0, caches=None):
    m = x.shape[0]
    steps = m // ROW_TILE
    tiles_per_cond = rows_per_cond // ROW_TILE
    row = lambda i: (i, 0)
    in_specs = [
        pl.BlockSpec((ROW_TILE, D_MODEL), row),
        pl.BlockSpec((None, None, 1, 6 * D_MODEL),
                     lambda i: (layer, cond0 + i // tiles_per_cond, 0, 0)),
        _resident((None, 1, D_MODEL), lambda i: (layer, 0, 0)),
        _resident((None, D_MODEL, IN_W), lambda i: (layer, 0, 0)),
    ]
    args = [x, mod, g1, w_in]
    half = lambda dt: (pl.BlockSpec((ROW_TILE, ATTN_W), row), jax.ShapeDtypeStruct((m, ATTN_W), dt))
    aliases = {}
    if caches is None:
        kv = [half(BF16), half(BF16)]
    else:
        batch = caches[0]
        cache_spec = pl.BlockSpec((None, None, ROW_TILE, ATTN_W), lambda i: (i, layer, 0, 0))
        cache_shape = jax.ShapeDtypeStruct((batch, DEPTH, ROW_TILE, ATTN_W), F32)
        kv = [(cache_spec, cache_shape), (cache_spec, cache_shape)]
        if caches[1] is not None:
            in_specs += [pl.BlockSpec(memory_space=pl.ANY)] * 2
            args += [caches[1], caches[2]]
            aliases = {4: 2, 5: 3}
    outs = [half(F32), half(BF16)] + kv + [
        half(F32), half(F32),
        (pl.BlockSpec((ROW_TILE, N_BRANCH * D_MODEL), row),
         jax.ShapeDtypeStruct((m, N_BRANCH * D_MODEL), F32)),
    ]
    return pl.pallas_call(
        _in_proj_kernel,
        grid=(steps,),
        in_specs=in_specs,
        out_specs=[o[0] for o in outs],
        out_shape=[o[1] for o in outs],
        input_output_aliases=aliases,
        compiler_params=_params(("arbitrary",), 48),
        name="in_proj",
    )(*args)


def _pool_kernel(u_ref, w_ref, s_ref, o_ref, *, seq, nseq):
    t = lax.broadcasted_iota(jnp.int32, (seq, POOL_GRP), 0)
    zeros = jnp.zeros((POOL_PAD, POOL_GRP), F32)
    n = seq + 2 * POOL_PAD
    for b in range(nseq):
        for g, win in enumerate(POOL_WINDOWS):
            half = win // 2
            cols = slice(g * POOL_GRP, (g + 1) * POOL_GRP)
            x = u_ref[b, :, cols]
            xe = jnp.concatenate([zeros, x, zeros], axis=0)
            s = pltpu.roll(xe, 1, 0) + xe
            span = 1
            while span < half:
                s = pltpu.roll(s, span, 0) + pltpu.roll(s, n - span, 0)
                span *= 2
            s = s[POOL_PAD:POOL_PAD + seq]
            cnt = (jnp.minimum(t + half, seq) - jnp.maximum(t - half, 0)).astype(F32)
            d = s / cnt - x
            y = _mm(d.astype(BF16), w_ref[g])
            o_ref[b, :, cols] = (y * s_ref[:, cols]).astype(o_ref.dtype)


def _pool(u_pool, pool_w, pool_scale, layer, *, nseq):
    b, seq, _ = u_pool.shape
    blk = pl.BlockSpec((nseq, seq, POOL_W), lambda i: (i, 0, 0))
    return pl.pallas_call(
        functools.partial(_pool_kernel, seq=seq, nseq=nseq),
        grid=(b // nseq,),
        in_specs=[
            blk,
            _resident((None, N_POOL, POOL_GRP, POOL_GRP), lambda i: (layer, 0, 0, 0)),
            _resident((None, 1, POOL_W), lambda i: (layer, 0, 0)),
        ],
        out_specs=blk,
        out_shape=jax.ShapeDtypeStruct(u_pool.shape, BF16),
        compiler_params=_params(("arbitrary",), 48),
        name="pool",
    )(u_pool, pool_w, pool_scale)


def _attn_ctx_kernel(q_ref, k_ref, v_ref, o_ref, *, nseq):
    for b in range(nseq):
        for h in range(N_HEADS):
            cols = slice(h * HEAD_DIM, (h + 1) * HEAD_DIM)
            q = q_ref[b, :, cols]
            k = k_ref[b, :, cols].astype(BF16)
            v = v_ref[b, :, cols].astype(BF16)
            s = _mm_nt(q, k)
            e = jnp.exp(s - jnp.max(s, axis=-1, keepdims=True))
            denom = jnp.sum(e, axis=-1, keepdims=True)
            o = _mm(e.astype(BF16), v) / denom
            o_ref[b, :, cols] = o.astype(o_ref.dtype)


def _attn_ctx(q, cache_k, cache_v, layer, *, nseq):
    b, seq, _ = q.shape
    blk = pl.BlockSpec((nseq, seq, ATTN_W), lambda i: (i, 0, 0))
    cache_blk = pl.BlockSpec((nseq, None, seq, ATTN_W), lambda i: (i, layer, 0, 0))
    return pl.pallas_call(
        functools.partial(_attn_ctx_kernel, nseq=nseq),
        grid=(b // nseq,),
        in_specs=[blk, cache_blk, cache_blk],
        out_specs=blk,
        out_shape=jax.ShapeDtypeStruct(q.shape, BF16),
        compiler_params=_params(("arbitrary",), 48),
        name="attn_ctx",
    )(q, cache_k, cache_v)


def _attn_lat_kernel(q_ref, k_ref, v_ref, ck_ref, cv_ref, bias_ref, o_ref, *, rows):
    j = pl.program_id(1)
    nsteps = pl.num_programs(1)
    span = NA_KH * GRID_W
    qcol = lax.broadcasted_iota(jnp.int32, (GRID_W, span), 0)
    kcol = lax.broadcasted_iota(jnp.int32, (GRID_W, span), 1) % GRID_W
    col0 = jnp.clip(qcol - NA_KW // 2, 0, GRID_W - NA_KW)
    col_ok = (kcol >= col0) & (kcol < col0 + NA_KW)
    edge = (j == 0) | (j == nsteps - 1)

    def one_row(i, carry):
        r = j * LAT_ROWS_PER_STEP + i
        row0 = jnp.clip(r - NA_KH // 2, 0, rows - NA_KH)
        kstart = pl.multiple_of(row0 * GRID_W, GRID_W)
        qstart = pl.multiple_of(i * GRID_W, GRID_W)
        variant = jnp.where(edge, i, 0)
        for h in range(N_HEADS):
            cols = slice(h * HEAD_DIM, (h + 1) * HEAD_DIM)
            q = q_ref[pl.ds(qstart, GRID_W), cols]
            kl = k_ref[pl.ds(kstart, span), cols]
            vl = v_ref[pl.ds(kstart, span), cols]
            s_lat = _mm_nt(q, kl) + bias_ref[variant, h]
            s_lat = jnp.where(col_ok, s_lat, NEG_INF)
            s_ctx = _mm_nt(q, ck_ref[:, cols])
            m = jnp.maximum(jnp.max(s_lat, axis=-1, keepdims=True),
                            jnp.max(s_ctx, axis=-1, keepdims=True))
            e_lat = jnp.exp(s_lat - m)
            e_ctx = jnp.exp(s_ctx - m)
            denom = (jnp.sum(e_lat, axis=-1, keepdims=True)
                     + jnp.sum(e_ctx, axis=-1, keepdims=True))
            o = (_mm(e_lat.astype(BF16), vl) + _mm(e_ctx.astype(BF16), cv_ref[:, cols])) / denom
            o_ref[pl.ds(qstart, GRID_W), cols] = o.astype(o_ref.dtype)
        return carry

    lax.fori_loop(0, LAT_ROWS_PER_STEP, one_row, 0)


def _attn_lat(q, k, v, ck, cv, bias, layer):
    b, seq, _ = q.shape
    rows = seq // GRID_W
    steps = rows // LAT_ROWS_PER_STEP
    tq = LAT_ROWS_PER_STEP * GRID_W
    qblk = pl.BlockSpec((None, tq, ATTN_W), lambda bi, j: (bi, j, 0))
    full = pl.BlockSpec((None, seq, ATTN_W), lambda bi, j: (bi, 0, 0))
    past = ck.shape[2]
    ctx = pl.BlockSpec((None, None, past, ATTN_W), lambda bi, j: (bi, layer, 0, 0))
    bias_blk = pl.BlockSpec((None, LAT_ROWS_PER_STEP, N_HEADS, GRID_W, NA_KH * GRID_W),
                            lambda bi, j: (layer, jnp.minimum(j, 1), 0, 0, 0))
    return pl.pallas_call(
        functools.partial(_attn_lat_kernel, rows=rows),
        grid=(b, steps),
        in_specs=[qblk, full, full, ctx, ctx, bias_blk],
        out_specs=qblk,
        out_shape=jax.ShapeDtypeStruct(q.shape, BF16),
        compiler_params=_params(("arbitrary", "arbitrary"), 48),
        name="attn_lat",
    )(q, k, v, ck, cv, bias)


def _na_bias_variants(na_rpb):
    cidx = jnp.arange(GRID_W)
    dc = jnp.clip(cidx[None, :] - cidx[:, None], -(NA_KW - 1), NA_KW - 1) + (NA_KW - 1)
    table = na_rpb[:, :, :, dc]
    variants = []
    for t in range(NA_KH):
        first = NA_KH - 1 - t
        v = table[:, :, first:first + NA_KH]
        variants.append(v.transpose(0, 1, 3, 2, 4).reshape(DEPTH, N_HEADS, GRID_W, NA_KH * GRID_W))
    return jnp.stack(variants, axis=1).astype(F32)


def _log_sigmoid(x):
    return jnp.minimum(x, 0.0) - jnp.log1p(jnp.exp(-jnp.abs(x)))


def _lru_kernel(ux_ref, uy_ref, h0_ref, cw_ref, cb_ref, wg_ref, ba_ref, bx_ref, lam_ref,
                y_ref, st_ref, xc_s, a_s, b_s, *, seq):
    chunk = seq // SCAN_CHUNKS
    pitch = chunk + SUBLANES
    nslab = LRU_W // LANES
    blk_rows = min(seq, ROW_TILE)
    nblk = seq // blk_rows

    zeros = jnp.zeros((CONV_PAD, LANES), F32)
    n = seq + 2 * CONV_PAD
    for s in range(nslab):
        cols = slice(s * LANES, (s + 1) * LANES)
        xe = jnp.concatenate([zeros, ux_ref[:, cols], zeros], axis=0)
        xc = cb_ref[:, cols] + cw_ref[0:1, cols] * pltpu.roll(xe, 2, 0)
        xc = xc + cw_ref[1:2, cols] * pltpu.roll(xe, 1, 0)
        xc = xc + cw_ref[2:3, cols] * xe
        xc = xc + cw_ref[3:4, cols] * pltpu.roll(xe, n - 1, 0)
        xc_s[:, cols] = xc[CONV_PAD:CONV_PAD + seq]

    log_lam = _log_sigmoid(lam_ref[...])

    def gates(blk, carry):
        r0 = pl.multiple_of(blk * blk_rows, blk_rows)
        xc = xc_s[pl.ds(r0, blk_rows), :]
        xb = xc.astype(BF16)
        for e in range(2):
            def gate(kind, bias):
                lo = _mm(xb[:, :MXU_DIM], wg_ref[e * 4 + kind * 2])
                hi = _mm(xb[:, MXU_DIM:], wg_ref[e * 4 + kind * 2 + 1])
                return jax.nn.sigmoid(jnp.concatenate([lo, hi], axis=1) + bias)
            rg = gate(0, ba_ref[e:e + 1, :])
            ig = gate(1, bx_ref[e:e + 1, :])
            log_a = LRU_C * rg * log_lam[e:e + 1, :]
            a = jnp.exp(log_a)
            th = jnp.tanh(log_a)
            inp = jnp.sqrt(-2.0 * th / (1.0 - th)) * ig * xc
            for s in range(nslab):
                cols = slice(s * LANES, (s + 1) * LANES)
                for c in range(blk_rows // chunk):
                    rows = slice(c * chunk, (c + 1) * chunk)
                    dst = pl.ds(pl.multiple_of((blk * (blk_rows // chunk) + c) * pitch, SUBLANES), chunk)
                    a_s[e, s, dst, :] = a[rows, cols]
                    b_s[e, s, dst, :] = inp[rows, cols]
        return carry

    lax.fori_loop(0, nblk, gates, 0)

    def strided(j):
        return pl.ds(j, SCAN_CHUNKS, stride=pitch)

    def position(e, j):
        return j if e == 0 else chunk - 1 - j

    def pass1(j, carry):
        out = []
        for e in range(2):
            idx = strided(position(e, j))
            for s in range(nslab):
                h, acc = carry[(e * nslab + s) * 2], carry[(e * nslab + s) * 2 + 1]
                a = a_s[e, s, idx, :]
                out += [a * h + b_s[e, s, idx, :], a * acc]
        return tuple(out)

    init = []
    for _ in range(2 * nslab):
        init += [jnp.zeros((SCAN_CHUNKS, LANES), F32), jnp.ones((SCAN_CHUNKS, LANES), F32)]
    agg = lax.fori_loop(0, chunk, pass1, tuple(init))

    starts = []
    for e in range(2):
        order = range(SCAN_CHUNKS) if e == 0 else range(SCAN_CHUNKS - 1, -1, -1)
        for s in range(nslab):
            cols = slice(s * LANES, (s + 1) * LANES)
            hend, atot = agg[(e * nslab + s) * 2], agg[(e * nslab + s) * 2 + 1]
            carry = h0_ref[e:e + 1, cols]
            rows = [None] * SCAN_CHUNKS
            for c in order:
                rows[c] = carry
                carry = atot[c:c + 1, :] * carry + hend[c:c + 1, :]
            st_ref[e:e + 1, cols] = carry
            starts.append(jnp.concatenate(rows, axis=0))

    def pass2(j, carry):
        out = []
        for e in range(2):
            idx = strided(position(e, j))
            for s in range(nslab):
                h = a_s[e, s, idx, :] * carry[e * nslab + s] + b_s[e, s, idx, :]
                b_s[e, s, idx, :] = h
                out.append(h)
        return tuple(out)

    lax.fori_loop(0, chunk, pass2, tuple(starts))

    def finish(blk, carry):
        for c in range(blk_rows // chunk):
            src = pl.ds(pl.multiple_of((blk * (blk_rows // chunk) + c) * pitch, SUBLANES), chunk)
            dst = pl.ds(pl.multiple_of(blk * blk_rows + c * chunk, chunk), chunk)
            for s in range(nslab):
                cols = slice(s * LANES, (s + 1) * LANES)
                hsum = b_s[0, s, src, :] + b_s[1, s, src, :]
                y_ref[dst, cols] = (hsum * jax.nn.gelu(uy_ref[dst, cols])).astype(y_ref.dtype)
        return carry

    lax.fori_loop(0, nblk, finish, 0)


def _lru(ux, uy, h0, h0_layer, conv_w, conv_b, wg, ba, bx, lam, layer):
    b, seq, _ = ux.shape
    chunk = seq // SCAN_CHUNKS
    pitch = chunk + SUBLANES
    blk = pl.BlockSpec((None, seq, LRU_W), lambda i: (i, 0, 0))
    per_layer = lambda shape: _resident((None,) + shape, lambda i: (layer,) + (0,) * len(shape))
    scan_buf = pltpu.VMEM((2, LRU_W // LANES, SCAN_CHUNKS * pitch, LANES), F32)
    return pl.pallas_call(
        functools.partial(_lru_kernel, seq=seq),
        grid=(b,),
        in_specs=[
            blk, blk,
            pl.BlockSpec((None, None, 2, LRU_W), lambda i: (i, h0_layer, 0, 0)),
            per_layer((CONV_W, LRU_W)),
            per_layer((1, LRU_W)),
            per_layer((8, MXU_DIM, MXU_DIM)),
            per_layer((2, LRU_W)),
            per_layer((2, LRU_W)),
            per_layer((2, LRU_W)),
        ],
        out_specs=[blk, pl.BlockSpec((None, 2, LRU_W), lambda i: (i, 0, 0))],
        out_shape=[jax.ShapeDtypeStruct(ux.shape, BF16), jax.ShapeDtypeStruct((b, 2, LRU_W), F32)],
        scratch_shapes=[pltpu.VMEM((seq, LRU_W), F32), scan_buf, scan_buf],
        compiler_params=_params(("arbitrary",), 56),
        name="rglru",
    )(ux, uy, h0, conv_w, conv_b, wg, ba, bx, lam)


def _gate_block_diag(w):
    per_tile = MXU_DIM // LRU_BLK
    w = w.reshape(DEPTH, 2, LRU_BLOCKS // per_tile, per_tile, LRU_BLK, LRU_BLK)
    eye = jnp.eye(per_tile, dtype=w.dtype)
    out = w[:, :, :, :, :, None, :] * eye[None, None, None, :, None, :, None]
    return out.reshape(DEPTH, 2, LRU_BLOCKS // per_tile, MXU_DIM, MXU_DIM)


def _out_proj_kernel(x_ref, ya_ref, yb_ref, yc_ref, ug_ref, mod_ref, g2_ref, wbr_ref, wout_ref,
                     wfi_ref, wfo_ref, gf_ref, o_ref, *, final):
    def mod(i):
        return mod_ref[:, i * D_MODEL:(i + 1) * D_MODEL]

    x = x_ref[...]
    merged = None
    for n, y_ref in enumerate((ya_ref, yb_ref, yc_ref)):
        proj = _mm(y_ref[...], wbr_ref[n])
        term = jax.nn.sigmoid(ug_ref[:, n * D_MODEL:(n + 1) * D_MODEL]) * proj
        merged = term if merged is None else merged + term
    x = x + mod(2) * _mm(merged.astype(BF16), wout_ref[...])

    h2 = (_rmsnorm(x, g2_ref[...]) * (1.0 + mod(4)) + mod(3)).astype(BF16)
    ffn = None
    for c in range(FF_HIDDEN // FF_CHUNK):
        lo = c * FF_CHUNK
        fg = _mm(h2, wfi_ref[:, lo:lo + FF_CHUNK])
        fu = _mm(h2, wfi_ref[:, FF_HIDDEN + lo:FF_HIDDEN + lo + FF_CHUNK])
        act = ((fg * jax.nn.sigmoid(fg)) * fu).astype(BF16)
        part = _mm(act, wfo_ref[lo:lo + FF_CHUNK, :])
        ffn = part if ffn is None else ffn + part
    x = x + mod(5) * ffn
    if final:
        x = _rmsnorm(x, gf_ref[...])
    o_ref[...] = x


def _out_proj(x, ya, yb, yc, ug, mod, g2, w_branch, w_out, w_ff_in, w_ff_out, g_final, layer,
              *, rows_per_cond, cond0):
    m = x.shape[0]
    tiles_per_cond = rows_per_cond // ROW_TILE
    row = lambda i: (i, 0)
    wide = pl.BlockSpec((ROW_TILE, D_MODEL), row)
    half = pl.BlockSpec((ROW_TILE, ATTN_W), row)
    per_layer = lambda shape: _resident((None,) + shape, lambda i: (layer,) + (0,) * len(shape))
    return pl.pallas_call(
        functools.partial(_out_proj_kernel, final=layer == DEPTH - 1),
        grid=(m // ROW_TILE,),
        in_specs=[
            wide, half, half, half,
            pl.BlockSpec((ROW_TILE, N_BRANCH * D_MODEL), row),
            pl.BlockSpec((None, None, 1, 6 * D_MODEL),
                         lambda i: (layer, cond0 + i // tiles_per_cond, 0, 0)),
            per_layer((1, D_MODEL)),
            per_layer((N_BRANCH, ATTN_W, D_MODEL)),
            per_layer((D_MODEL, D_MODEL)),
            per_layer((D_MODEL, 2 * FF_HIDDEN)),
            per_layer((FF_HIDDEN, D_MODEL)),
            _resident((1, D_MODEL), lambda i: (0, 0)),
        ],
        out_specs=wide,
        out_shape=jax.ShapeDtypeStruct((m, D_MODEL), F32),
        compiler_params=_params(("arbitrary",), 56),
        name="out_proj",
    )(x, ya, yb, yc, ug, mod, g2, w_branch, w_out, w_ff_in, w_ff_out, g_final)


def kernel(x_prompt, x_sample, cache_k, cache_v, state_lru, c, c_ctx, w_mod, b_mod, g_norm1, g_norm2, w_in, pool_w, pool_scale, na_rpb, lru_conv_w, lru_conv_b, lru_wa, lru_ba, lru_wx, lru_bx, lru_lambda, w_branch, w_out, w_ff_in, w_ff_out, g_final):
    batch, seq, _ = x_prompt.shape
    dec_batch, dec_seq, _ = x_sample.shape
    past = cache_k.shape[2]

    cond = jnp.concatenate([c_ctx[None, :], c,
                            jnp.zeros((SUBLANES - 1 - dec_batch, D_MODEL), F32)], axis=0)
    mod = _modulation(cond, w_mod, b_mod)[:, :1 + dec_batch].reshape(DEPTH, 1 + dec_batch, 1, 6 * D_MODEL)

    w_in_b = w_in.astype(BF16)
    pool_w_b = pool_w.astype(BF16)
    w_branch_b = w_branch.astype(BF16)
    w_out_b = w_out.astype(BF16)
    w_ff_in_b = w_ff_in.astype(BF16)
    w_ff_out_b = w_ff_out.astype(BF16)
    wg = jnp.stack([_gate_block_diag(lru_wa), _gate_block_diag(lru_wx)], axis=2)
    wg = wg.reshape(DEPTH, 8, MXU_DIM, MXU_DIM).astype(BF16)
    g1 = g_norm1.reshape(DEPTH, 1, D_MODEL)
    g2 = g_norm2.reshape(DEPTH, 1, D_MODEL)
    gf = g_final.reshape(1, D_MODEL)
    pscale = pool_scale.reshape(DEPTH, 1, POOL_W)
    conv_b = lru_conv_b.reshape(DEPTH, 1, LRU_W)
    ck = cache_k.reshape(dec_batch, DEPTH, past, ATTN_W).astype(BF16)
    cv = cache_v.reshape(dec_batch, DEPTH, past, ATTN_W).astype(BF16)
    bias = _na_bias_variants(na_rpb)
    h0_ctx = jnp.zeros((batch, 1, 2, LRU_W), F32)

    xc = x_prompt.reshape(batch * seq, D_MODEL)
    xs = x_sample.reshape(dec_batch * dec_seq, D_MODEL)
    new_k = new_v = None
    states = []
    for l in range(DEPTH):
        lru_args = (lru_conv_w, conv_b, wg, lru_ba, lru_bx, lru_lambda, l)
        out_args = (mod, g2, w_branch_b, w_out_b, w_ff_in_b, w_ff_out_b, gf, l)

        up, q, new_k, new_v, ux, uy, ug = _in_proj(
            xc, mod, g1, w_in_b, l, rows_per_cond=batch * seq, cond0=0, caches=(batch, new_k, new_v))
        s3 = lambda a: a.reshape(batch, seq, a.shape[-1])
        ya = _pool(s3(up), pool_w_b, pscale, l, nseq=4)
        yb = _attn_ctx(s3(q), new_k, new_v, l, nseq=4)
        yc, st = _lru(s3(ux), s3(uy), h0_ctx, 0, *lru_args)
        states.append(st)
        flat = lambda a: a.reshape(-1, a.shape[-1])
        xc = _out_proj(xc, flat(ya), flat(yb), flat(yc), ug, *out_args,
                       rows_per_cond=batch * seq, cond0=0)

        up, q, k, v, ux, uy, ug = _in_proj(
            xs, mod, g1, w_in_b, l, rows_per_cond=dec_seq, cond0=1)
        d3 = lambda a: a.reshape(dec_batch, dec_seq, a.shape[-1])
        ya = _pool(d3(up), pool_w_b, pscale, l, nseq=1)
        yb = _attn_lat(d3(q), d3(k), d3(v), ck, cv, bias, l)
        yc, _ = _lru(d3(ux), d3(uy), state_lru, l, *lru_args)
        xs = _out_proj(xs, flat(ya), flat(yb), flat(yc), ug, *out_args,
                       rows_per_cond=dec_seq, cond0=1)

    y_prompt = xc.reshape(batch, seq, D_MODEL)
    y_sample = xs.reshape(dec_batch, dec_seq, D_MODEL)
    new_cache_k = new_k.reshape(batch, DEPTH, seq, N_HEADS, HEAD_DIM)
    new_cache_v = new_v.reshape(batch, DEPTH, seq, N_HEADS, HEAD_DIM)
    new_state_lru = jnp.stack(states, axis=1)
    return (y_prompt, y_sample, new_cache_k, new_cache_v, new_state_lru)
```

```python
import functools

import jax
import jax.numpy as jnp
import numpy as np
from jax import lax
from jax.experimental import pallas as pl
from jax.experimental.pallas import tpu as pltpu

D_MODEL = 1024
DEPTH = 4
GRID_W = 64
N_HEADS = 8
HEAD_DIM = 64
ATTN_W = N_HEADS * HEAD_DIM
NA_KH = 8
NA_KW = 16
POOL_WINDOWS = (2, 4, 8, 16)
N_POOL = 4
POOL_GRP = 128
POOL_W = N_POOL * POOL_GRP
LRU_W = 512
LRU_BLOCKS = 8
LRU_BLK = LRU_W // LRU_BLOCKS
LRU_C = 8.0
CONV_W = 4
N_BRANCH = 3
FF_HIDDEN = 2816
IN_W = POOL_W + 3 * ATTN_W + 2 * LRU_W + N_BRANCH * D_MODEL
EPS = 1e-6
NEG_INF = -1e30
ATTN_SCALE = HEAD_DIM ** -0.5

BF16 = jnp.bfloat16
F32 = jnp.float32

LANES = 128
SUBLANES = 8
MXU_DIM = 256
MIB = 1024 * 1024

ROW_TILE = 256
FF_CHUNK = 1408
MOD_COLS = 1536
POOL_PAD = 16
CONV_PAD = 8
SCAN_CHUNKS = SUBLANES
SCAN_UNROLL = 4
LAT_ROWS_PER_STEP = 4


def _params(semantics, vmem_mib):
    return pltpu.CompilerParams(dimension_semantics=semantics,
                                vmem_limit_bytes=vmem_mib * MIB)


def _resident(block_shape, index_map):
    return pl.BlockSpec(block_shape, index_map, pipeline_mode=pl.Buffered(1))


def _mm(a, b):
    return jnp.dot(a, b, preferred_element_type=F32)


def _mm_nt(a, b):
    return lax.dot_general(a, b, (((1,), (1,)), ((), ())), preferred_element_type=F32)


def _rmsnorm(x, g):
    ms = jnp.mean(x * x, axis=-1, keepdims=True)
    return (x * lax.rsqrt(ms + EPS)) * g


def _mod_kernel(c_ref, w_ref, b_ref, o_ref):
    c = c_ref[...]
    s = c * jax.nn.sigmoid(c)
    o_ref[...] = jnp.dot(s, w_ref[...], preferred_element_type=F32,
                         precision=lax.Precision.HIGHEST) + b_ref[...]


def _modulation(cond, w_mod, b_mod):
    n = 6 * D_MODEL
    return pl.pallas_call(
        _mod_kernel,
        grid=(DEPTH, n // MOD_COLS),
        in_specs=[
            pl.BlockSpec((SUBLANES, D_MODEL), lambda l, j: (0, 0)),
            pl.BlockSpec((None, D_MODEL, MOD_COLS), lambda l, j: (l, 0, j)),
            pl.BlockSpec((None, 1, MOD_COLS), lambda l, j: (l, 0, j)),
        ],
        out_specs=pl.BlockSpec((None, SUBLANES, MOD_COLS), lambda l, j: (l, 0, j)),
        out_shape=jax.ShapeDtypeStruct((DEPTH, SUBLANES, n), F32),
        compiler_params=_params(("arbitrary", "arbitrary"), 32),
        name="modulation",
    )(cond, w_mod, b_mod.reshape(DEPTH, 1, n))


def _in_proj_kernel(x_ref, mod_ref, g_ref, w_ref, *out_refs):
    up_ref, q_ref, k_ref, v_ref, ux_ref, uy_ref, ug_ref = out_refs[-7:]
    x = x_ref[...]
    sh1 = mod_ref[:, 0:D_MODEL]
    sc1 = mod_ref[:, D_MODEL:2 * D_MODEL]
    h = (_rmsnorm(x, g_ref[...]) * (1.0 + sc1) + sh1).astype(BF16)

    def seg(lo, width):
        return _mm(h, w_ref[:, lo:lo + width])

    off = 0
    up_ref[...] = seg(off, POOL_W)
    off += POOL_W
    q_ref[...] = (seg(off, ATTN_W) * ATTN_SCALE).astype(q_ref.dtype)
    off += ATTN_W
    k_ref[...] = seg(off, ATTN_W).astype(k_ref.dtype)
    off += ATTN_W
    v_ref[...] = seg(off, ATTN_W).astype(v_ref.dtype)
    off += ATTN_W
    ux_ref[...] = seg(off, LRU_W)
    off += LRU_W
    uy_ref[...] = seg(off, LRU_W)
    off += LRU_W
    for n in range(N_BRANCH):
        ug_ref[:, n * D_MODEL:(n + 1) * D_MODEL] = seg(off + n * D_MODEL, D_MODEL)


def _in_proj(x, mod, g1, w_in, layer, *, rows_per_cond, cond0, caches=None):
    m = x.shape[0]
    steps = m // ROW_TILE
    tiles_per_cond = rows_per_cond // ROW_TILE
    row = lambda i: (i, 0)
    in_specs = [
        pl.BlockSpec((ROW_TILE, D_MODEL), row),
        pl.BlockSpec((None, None, 1, 6 * D_MODEL),
                     lambda i: (layer, cond0 + i // tiles_per_cond, 0, 0)),
        _resident((None, 1, D_MODEL), lambda i: (layer, 0, 0)),
        _resident((None, D_MODEL, IN_W), lambda i: (layer, 0, 0)),
    ]
    args = [x, mod, g1, w_in]
    half = lambda dt: (pl.BlockSpec((ROW_TILE, ATTN_W), row), jax.ShapeDtypeStruct((m, ATTN_W), dt))
    aliases = {}
    if caches is None:
        kv = [half(BF16), half(BF16)]
    else:
        batch = caches[0]
        cache_spec = pl.BlockSpec((None, None, ROW_TILE, ATTN_W), lambda i: (i, layer, 0, 0))
        cache_shape = jax.ShapeDtypeStruct((batch, DEPTH, ROW_TILE, ATTN_W), F32)
        kv = [(cache_spec, cache_shape), (cache_spec, cache_shape)]
        if caches[1] is not None:
            in_specs += [pl.BlockSpec(memory_space=pl.ANY)] * 2
            args += [caches[1], caches[2]]
            aliases = {4: 2, 5: 3}
    outs = [half(F32), half(BF16)] + kv + [
        half(F32), half(F32),
        (pl.BlockSpec((ROW_TILE, N_BRANCH * D_MODEL), row),
         jax.ShapeDtypeStruct((m, N_BRANCH * D_MODEL), F32)),
    ]
    return pl.pallas_call(
        _in_proj_kernel,
        grid=(steps,),
        in_specs=in_specs,
        out_specs=[o[0] for o in outs],
        out_shape=[o[1] for o in outs],
        input_output_aliases=aliases,
        compiler_params=_params(("arbitrary",), 48),
        name="in_proj",
    )(*args)


def _pool_kernel(u_ref, w_ref, s_ref, o_ref, *, seq, nseq):
    t = lax.broadcasted_iota(jnp.int32, (seq, POOL_GRP), 0)
    zeros = jnp.zeros((POOL_PAD, POOL_GRP), F32)
    n = seq + 2 * POOL_PAD
    for b in range(nseq):
        for g, win in enumerate(POOL_WINDOWS):
            half = win // 2
            cols = slice(g * POOL_GRP, (g + 1) * POOL_GRP)
            x = u_ref[b, :, cols]
            xe = jnp.concatenate([zeros, x, zeros], axis=0)
            s = pltpu.roll(xe, 1, 0) + xe
            span = 1
            while span < half:
                s = pltpu.roll(s, span, 0) + pltpu.roll(s, n - span, 0)
                span *= 2
            s = s[POOL_PAD:POOL_PAD + seq]
            cnt = (jnp.minimum(t + half, seq) - jnp.maximum(t - half, 0)).astype(F32)
            d = s / cnt - x
            y = _mm(d.astype(BF16), w_ref[g])
            o_ref[b, :, cols] = (y * s_ref[:, cols]).astype(o_ref.dtype)


def _pool(u_pool, pool_w, pool_scale, layer, *, nseq):
    b, seq, _ = u_pool.shape
    blk = pl.BlockSpec((nseq, seq, POOL_W), lambda i: (i, 0, 0))
    return pl.pallas_call(
        functools.partial(_pool_kernel, seq=seq, nseq=nseq),
        grid=(b // nseq,),
        in_specs=[
            blk,
            _resident((None, N_POOL, POOL_GRP, POOL_GRP), lambda i: (layer, 0, 0, 0)),
            _resident((None, 1, POOL_W), lambda i: (layer, 0, 0)),
        ],
        out_specs=blk,
        out_shape=jax.ShapeDtypeStruct(u_pool.shape, BF16),
        compiler_params=_params(("arbitrary",), 48),
        name="pool",
    )(u_pool, pool_w, pool_scale)


def _first_head_lanes(rows):
    return lax.broadcasted_iota(jnp.int32, (rows, LANES), 1) < HEAD_DIM


def _attn_ctx_kernel(q_ref, k_ref, v_ref, o_ref, *, nseq):
    seq = q_ref.shape[1]
    first = _first_head_lanes(seq)
    ones = jnp.ones((seq, LANES), BF16)
    outs = []
    for b in range(nseq):
        pairs = []
        for p in range(ATTN_W // LANES):
            cols = slice(p * LANES, (p + 1) * LANES)
            q = q_ref[b, :, cols]
            k = k_ref[b, :, cols].astype(BF16)
            v_aug = jnp.concatenate([v_ref[b, :, cols].astype(BF16), ones], axis=1)
            halves = []
            for keep in (first, jnp.logical_not(first)):
                s = _mm_nt(jnp.where(keep, q, jnp.zeros_like(q)), k)
                e = jnp.exp(s - jnp.max(s, axis=-1, keepdims=True))
                ov = _mm(e.astype(BF16), v_aug)
                halves.append(ov[:, :LANES] / ov[:, LANES:])
            pairs.append(jnp.where(first, halves[0], halves[1]))
        outs.append(jnp.concatenate(pairs, axis=1).astype(o_ref.dtype))
    for b in range(nseq):
        o_ref[b] = outs[b]


def _attn_ctx(q, cache_k, cache_v, layer, *, nseq):
    b, seq, _ = q.shape
    blk = pl.BlockSpec((nseq, seq, ATTN_W), lambda i: (i, 0, 0))
    cache_blk = pl.BlockSpec((nseq, None, seq, ATTN_W), lambda i: (i, layer, 0, 0))
    return pl.pallas_call(
        functools.partial(_attn_ctx_kernel, nseq=nseq),
        grid=(b // nseq,),
        in_specs=[blk, cache_blk, cache_blk],
        out_specs=blk,
        out_shape=jax.ShapeDtypeStruct(q.shape, BF16),
        compiler_params=_params(("arbitrary",), 48),
        name="attn_ctx",
    )(q, cache_k, cache_v)


def _attn_lat_kernel(q_ref, k_ref, v_ref, ck_ref, cv_ref, bias_ref, o_ref, *, rows):
    j = pl.program_id(1)
    nsteps = pl.num_programs(1)
    span = NA_KH * GRID_W
    qcol = lax.broadcasted_iota(jnp.int32, (GRID_W, span), 0)
    kcol = lax.broadcasted_iota(jnp.int32, (GRID_W, span), 1) % GRID_W
    col0 = jnp.clip(qcol - NA_KW // 2, 0, GRID_W - NA_KW)
    col_ok = (kcol >= col0) & (kcol < col0 + NA_KW)
    edge = (j == 0) | (j == nsteps - 1)
    first = _first_head_lanes(GRID_W)
    ones_lat = jnp.ones((span, LANES), BF16)
    ones_ctx = jnp.ones((ck_ref.shape[0], LANES), BF16)

    def one_row(i, carry):
        r = j * LAT_ROWS_PER_STEP + i
        row0 = jnp.clip(r - NA_KH // 2, 0, rows - NA_KH)
        kstart = pl.multiple_of(row0 * GRID_W, GRID_W)
        qstart = pl.multiple_of(i * GRID_W, GRID_W)
        variant = jnp.where(edge, i, 0)
        pairs = []
        for p in range(ATTN_W // LANES):
            cols = slice(p * LANES, (p + 1) * LANES)
            q = q_ref[pl.ds(qstart, GRID_W), cols]
            kl = k_ref[pl.ds(kstart, span), cols]
            kc = ck_ref[:, cols]
            vl_aug = jnp.concatenate([v_ref[pl.ds(kstart, span), cols], ones_lat], axis=1)
            vc_aug = jnp.concatenate([cv_ref[:, cols], ones_ctx], axis=1)
            halves = []
            for half, keep in enumerate((first, jnp.logical_not(first))):
                qh = jnp.where(keep, q, jnp.zeros_like(q))
                s_lat = _mm_nt(qh, kl) + bias_ref[variant, 2 * p + half]
                s_lat = jnp.where(col_ok, s_lat, NEG_INF)
                s_ctx = _mm_nt(qh, kc)
                m = jnp.maximum(jnp.max(s_lat, axis=-1, keepdims=True),
                                jnp.max(s_ctx, axis=-1, keepdims=True))
                e_lat = jnp.exp(s_lat - m).astype(BF16)
                e_ctx = jnp.exp(s_ctx - m).astype(BF16)
                ov = _mm(e_lat, vl_aug) + _mm(e_ctx, vc_aug)
                halves.append(ov[:, :LANES] / ov[:, LANES:])
            pairs.append(jnp.where(first, halves[0], halves[1]))
        o_ref[pl.ds(qstart, GRID_W), :] = jnp.concatenate(pairs, axis=1).astype(o_ref.dtype)
        return carry

    lax.fori_loop(0, LAT_ROWS_PER_STEP, one_row, 0)


def _attn_lat(q, k, v, ck, cv, bias, layer):
    b, seq, _ = q.shape
    rows = seq // GRID_W
    steps = rows // LAT_ROWS_PER_STEP
    tq = LAT_ROWS_PER_STEP * GRID_W
    qblk = pl.BlockSpec((None, tq, ATTN_W), lambda bi, j: (bi, j, 0))
    full = pl.BlockSpec((None, seq, ATTN_W), lambda bi, j: (bi, 0, 0))
    past = ck.shape[2]
    ctx = pl.BlockSpec((None, None, past, ATTN_W), lambda bi, j: (bi, layer, 0, 0))
    bias_blk = pl.BlockSpec((None, LAT_ROWS_PER_STEP, N_HEADS, GRID_W, NA_KH * GRID_W),
                            lambda bi, j: (layer, jnp.minimum(j, 1), 0, 0, 0))
    return pl.pallas_call(
        functools.partial(_attn_lat_kernel, rows=rows),
        grid=(b, steps),
        in_specs=[qblk, full, full, ctx, ctx, bias_blk],
        out_specs=qblk,
        out_shape=jax.ShapeDtypeStruct(q.shape, BF16),
        compiler_params=_params(("arbitrary", "arbitrary"), 48),
        name="attn_lat",
    )(q, k, v, ck, cv, bias)


def _na_bias_variants(na_rpb):
    cidx = np.arange(GRID_W)
    dc = np.clip(cidx[None, :] - cidx[:, None], -(NA_KW - 1), NA_KW - 1) + (NA_KW - 1)
    pick = (np.arange(2 * NA_KW - 1)[:, None, None] == dc[None]).astype(np.float32)
    table = jnp.einsum('lhdj,jqk->lhqdk', na_rpb, pick, precision=lax.Precision.HIGHEST)
    variants = []
    for t in range(NA_KH):
        first = NA_KH - 1 - t
        v = table[:, :, :, first:first + NA_KH]
        variants.append(v.reshape(DEPTH, N_HEADS, GRID_W, NA_KH * GRID_W))
    return jnp.stack(variants, axis=1)


def _log_sigmoid(x):
    return jnp.minimum(x, 0.0) - jnp.log1p(jnp.exp(-jnp.abs(x)))


def _lru_kernel(ux_ref, uy_ref, h0_ref, cw_ref, cb_ref, wg_ref, ba_ref, bx_ref, lam_ref,
                y_ref, st_ref, xc_s, a_s, b_s, *, seq):
    chunk = seq // SCAN_CHUNKS
    pitch = chunk + SUBLANES
    nslab = LRU_W // LANES
    blk_rows = min(seq, ROW_TILE)
    nblk = seq // blk_rows

    zeros = jnp.zeros((CONV_PAD, LANES), F32)
    n = seq + 2 * CONV_PAD
    for s in range(nslab):
        cols = slice(s * LANES, (s + 1) * LANES)
        xe = jnp.concatenate([zeros, ux_ref[:, cols], zeros], axis=0)
        xc = cb_ref[:, cols] + cw_ref[0:1, cols] * pltpu.roll(xe, 2, 0)
        xc = xc + cw_ref[1:2, cols] * pltpu.roll(xe, 1, 0)
        xc = xc + cw_ref[2:3, cols] * xe
        xc = xc + cw_ref[3:4, cols] * pltpu.roll(xe, n - 1, 0)
        xc_s[:, cols] = xc[CONV_PAD:CONV_PAD + seq]

    decay = -LRU_C * _log_sigmoid(lam_ref[...])

    def gates(blk, carry):
        r0 = pl.multiple_of(blk * blk_rows, blk_rows)
        xc = xc_s[pl.ds(r0, blk_rows), :]
        xb = xc.astype(BF16)
        for e in range(2):
            def gate(kind, bias):
                lo = _mm(xb[:, :MXU_DIM], wg_ref[e * 4 + kind * 2])
                hi = _mm(xb[:, MXU_DIM:], wg_ref[e * 4 + kind * 2 + 1])
                return jax.nn.sigmoid(jnp.concatenate([lo, hi], axis=1) + bias)
            rg = gate(0, ba_ref[e:e + 1, :])
            ig = gate(1, bx_ref[e:e + 1, :])
            neg_log_a = rg * decay[e:e + 1, :]
            a = jnp.exp(-neg_log_a)
            z = jnp.tanh(neg_log_a) * (a * a + 1.0)
            root = jnp.where(z > 0.0, z * lax.rsqrt(z), 0.0)
            inp = root * ig * xc
            for s in range(nslab):
                cols = slice(s * LANES, (s + 1) * LANES)
                for c in range(blk_rows // chunk):
                    rows = slice(c * chunk, (c + 1) * chunk)
                    dst = pl.ds(pl.multiple_of((blk * (blk_rows // chunk) + c) * pitch, SUBLANES), chunk)
                    a_s[e, s, dst, :] = a[rows, cols]
                    b_s[e, s, dst, :] = inp[rows, cols]
        return carry

    lax.fori_loop(0, nblk, gates, 0)

    def strided(j):
        return pl.ds(j, SCAN_CHUNKS, stride=pitch)

    def position(e, j):
        return j if e == 0 else chunk - 1 - j

    def pass1(j, carry):
        out = []
        for e in range(2):
            idx = strided(position(e, j))
            for s in range(nslab):
                h, acc = carry[(e * nslab + s) * 2], carry[(e * nslab + s) * 2 + 1]
                a = a_s[e, s, idx, :]
                out += [a * h + b_s[e, s, idx, :], a * acc]
        return tuple(out)

    init = []
    for _ in range(2 * nslab):
        init += [jnp.zeros((SCAN_CHUNKS, LANES), F32), jnp.ones((SCAN_CHUNKS, LANES), F32)]
    agg = lax.fori_loop(0, chunk, pass1, tuple(init), unroll=SCAN_UNROLL)

    starts = []
    for e in range(2):
        order = range(SCAN_CHUNKS) if e == 0 else range(SCAN_CHUNKS - 1, -1, -1)
        for s in range(nslab):
            cols = slice(s * LANES, (s + 1) * LANES)
            hend, atot = agg[(e * nslab + s) * 2], agg[(e * nslab + s) * 2 + 1]
            carry = h0_ref[e:e + 1, cols]
            rows = [None] * SCAN_CHUNKS
            for c in order:
                rows[c] = carry
                carry = atot[c:c + 1, :] * carry + hend[c:c + 1, :]
            st_ref[e:e + 1, cols] = carry
            starts.append(jnp.concatenate(rows, axis=0))

    def pass2(j, carry):
        out = []
        for e in range(2):
            idx = strided(position(e, j))
            for s in range(nslab):
                h = a_s[e, s, idx, :] * carry[e * nslab + s] + b_s[e, s, idx, :]
                b_s[e, s, idx, :] = h
                out.append(h)
        return tuple(out)

    lax.fori_loop(0, chunk, pass2, tuple(starts), unroll=SCAN_UNROLL)

    def finish(blk, carry):
        for c in range(blk_rows // chunk):
            src = pl.ds(pl.multiple_of((blk * (blk_rows // chunk) + c) * pitch, SUBLANES), chunk)
            dst = pl.ds(pl.multiple_of(blk * blk_rows + c * chunk, chunk), chunk)
            for s in range(nslab):
                cols = slice(s * LANES, (s + 1) * LANES)
                hsum = b_s[0, s, src, :] + b_s[1, s, src, :]
                y_ref[dst, cols] = (hsum * jax.nn.gelu(uy_ref[dst, cols])).astype(y_ref.dtype)
        return carry

    lax.fori_loop(0, nblk, finish, 0)


def _lru(ux, uy, h0, h0_layer, conv_w, conv_b, wg, ba, bx, lam, layer):
    b, seq, _ = ux.shape
    chunk = seq // SCAN_CHUNKS
    pitch = chunk + SUBLANES
    blk = pl.BlockSpec((None, seq, LRU_W), lambda i: (i, 0, 0))
    per_layer = lambda shape: _resident((None,) + shape, lambda i: (layer,) + (0,) * len(shape))
    scan_buf = pltpu.VMEM((2, LRU_W // LANES, SCAN_CHUNKS * pitch, LANES), F32)
    return pl.pallas_call(
        functools.partial(_lru_kernel, seq=seq),
        grid=(b,),
        in_specs=[
            blk, blk,
            pl.BlockSpec((None, None, 2, LRU_W), lambda i: (i, h0_layer, 0, 0)),
            per_layer((CONV_W, LRU_W)),
            per_layer((1, LRU_W)),
            per_layer((8, MXU_DIM, MXU_DIM)),
            per_layer((2, LRU_W)),
            per_layer((2, LRU_W)),
            per_layer((2, LRU_W)),
        ],
        out_specs=[blk, pl.BlockSpec((None, 2, LRU_W), lambda i: (i, 0, 0))],
        out_shape=[jax.ShapeDtypeStruct(ux.shape, BF16), jax.ShapeDtypeStruct((b, 2, LRU_W), F32)],
        scratch_shapes=[pltpu.VMEM((seq, LRU_W), F32), scan_buf, scan_buf],
        compiler_params=_params(("arbitrary",), 56),
        name="rglru",
    )(ux, uy, h0, conv_w, conv_b, wg, ba, bx, lam)


def _gate_block_diag(w):
    per_tile = MXU_DIM // LRU_BLK
    w = w.reshape(DEPTH, 2, LRU_BLOCKS // per_tile, per_tile, LRU_BLK, LRU_BLK)
    eye = jnp.eye(per_tile, dtype=w.dtype)
    out = w[:, :, :, :, :, None, :] * eye[None, None, None, :, None, :, None]
    return out.reshape(DEPTH, 2, LRU_BLOCKS // per_tile, MXU_DIM, MXU_DIM)


def _out_proj_kernel(x_ref, ya_ref, yb_ref, yc_ref, ug_ref, mod_ref, g2_ref, wbr_ref, wout_ref,
                     wfi_ref, wfo_ref, gf_ref, o_ref, *, final):
    def mod(i):
        return mod_ref[:, i * D_MODEL:(i + 1) * D_MODEL]

    x = x_ref[...]
    merged = None
    for n, y_ref in enumerate((ya_ref, yb_ref, yc_ref)):
        proj = _mm(y_ref[...], wbr_ref[n])
        term = jax.nn.sigmoid(ug_ref[:, n * D_MODEL:(n + 1) * D_MODEL]) * proj
        merged = term if merged is None else merged + term
    x = x + mod(2) * _mm(merged.astype(BF16), wout_ref[...])

    h2 = (_rmsnorm(x, g2_ref[...]) * (1.0 + mod(4)) + mod(3)).astype(BF16)
    ffn = None
    for c in range(FF_HIDDEN // FF_CHUNK):
        lo = c * FF_CHUNK
        fg = _mm(h2, wfi_ref[:, lo:lo + FF_CHUNK])
        fu = _mm(h2, wfi_ref[:, FF_HIDDEN + lo:FF_HIDDEN + lo + FF_CHUNK])
        act = ((fg * jax.nn.sigmoid(fg)) * fu).astype(BF16)
        part = _mm(act, wfo_ref[lo:lo + FF_CHUNK, :])
        ffn = part if ffn is None else ffn + part
    x = x + mod(5) * ffn
    if final:
        x = _rmsnorm(x, gf_ref[...])
    o_ref[...] = x


def _out_proj(x, ya, yb, yc, ug, mod, g2, w_branch, w_out, w_ff_in, w_ff_out, g_final, layer,
              *, rows_per_cond, cond0):
    m = x.shape[0]
    tiles_per_cond = rows_per_cond // ROW_TILE
    row = lambda i: (i, 0)
    wide = pl.BlockSpec((ROW_TILE, D_MODEL), row)
    half = pl.BlockSpec((ROW_TILE, ATTN_W), row)
    per_layer = lambda shape: _resident((None,) + shape, lambda i: (layer,) + (0,) * len(shape))
    return pl.pallas_call(
        functools.partial(_out_proj_kernel, final=layer == DEPTH - 1),
        grid=(m // ROW_TILE,),
        in_specs=[
            wide, half, half, half,
            pl.BlockSpec((ROW_TILE, N_BRANCH * D_MODEL), row),
            pl.BlockSpec((None, None, 1, 6 * D_MODEL),
                         lambda i: (layer, cond0 + i // tiles_per_cond, 0, 0)),
            per_layer((1, D_MODEL)),
            per_layer((N_BRANCH, ATTN_W, D_MODEL)),
            per_layer((D_MODEL, D_MODEL)),
            per_layer((D_MODEL, 2 * FF_HIDDEN)),
            per_layer((FF_HIDDEN, D_MODEL)),
            _resident((1, D_MODEL), lambda i: (0, 0)),
        ],
        out_specs=wide,
        out_shape=jax.ShapeDtypeStruct((m, D_MODEL), F32),
        compiler_params=_params(("arbitrary",), 56),
        name="out_proj",
    )(x, ya, yb, yc, ug, mod, g2, w_branch, w_out, w_ff_in, w_ff_out, g_final)


def kernel(x_prompt, x_sample, cache_k, cache_v, state_lru, c, c_ctx, w_mod, b_mod, g_norm1, g_norm2, w_in, pool_w, pool_scale, na_rpb, lru_conv_w, lru_conv_b, lru_wa, lru_ba, lru_wx, lru_bx, lru_lambda, w_branch, w_out, w_ff_in, w_ff_out, g_final):
    batch, seq, _ = x_prompt.shape
    dec_batch, dec_seq, _ = x_sample.shape
    past = cache_k.shape[2]

    cond = jnp.concatenate([c_ctx[None, :], c,
                            jnp.zeros((SUBLANES - 1 - dec_batch, D_MODEL), F32)], axis=0)
    mod = _modulation(cond, w_mod, b_mod)[:, :1 + dec_batch].reshape(DEPTH, 1 + dec_batch, 1, 6 * D_MODEL)

    w_in_b = w_in.astype(BF16)
    pool_w_b = pool_w.astype(BF16)
    w_branch_b = w_branch.astype(BF16)
    w_out_b = w_out.astype(BF16)
    w_ff_in_b = w_ff_in.astype(BF16)
    w_ff_out_b = w_ff_out.astype(BF16)
    wg = jnp.stack([_gate_block_diag(lru_wa), _gate_block_diag(lru_wx)], axis=2)
    wg = wg.reshape(DEPTH, 8, MXU_DIM, MXU_DIM).astype(BF16)
    g1 = g_norm1.reshape(DEPTH, 1, D_MODEL)
    g2 = g_norm2.reshape(DEPTH, 1, D_MODEL)
    gf = g_final.reshape(1, D_MODEL)
    pscale = pool_scale.reshape(DEPTH, 1, POOL_W)
    conv_b = lru_conv_b.reshape(DEPTH, 1, LRU_W)
    ck = cache_k.reshape(dec_batch, DEPTH, past, ATTN_W).astype(BF16)
    cv = cache_v.reshape(dec_batch, DEPTH, past, ATTN_W).astype(BF16)
    bias = _na_bias_variants(na_rpb)
    h0_ctx = jnp.zeros((batch, 1, 2, LRU_W), F32)

    xc = x_prompt.reshape(batch * seq, D_MODEL)
    xs = x_sample.reshape(dec_batch * dec_seq, D_MODEL)
    new_k = new_v = None
    states = []
    for l in range(DEPTH):
        lru_args = (lru_conv_w, conv_b, wg, lru_ba, lru_bx, lru_lambda, l)
        out_args = (mod, g2, w_branch_b, w_out_b, w_ff_in_b, w_ff_out_b, gf, l)

        up, q, new_k, new_v, ux, uy, ug = _in_proj(
            xc, mod, g1, w_in_b, l, rows_per_cond=batch * seq, cond0=0, caches=(batch, new_k, new_v))
        s3 = lambda a: a.reshape(batch, seq, a.shape[-1])
        ya = _pool(s3(up), pool_w_b, pscale, l, nseq=4)
        yb = _attn_ctx(s3(q), new_k, new_v, l, nseq=4)
        yc, st = _lru(s3(ux), s3(uy), h0_ctx, 0, *lru_args)
        states.append(st)
        flat = lambda a: a.reshape(-1, a.shape[-1])
        xc = _out_proj(xc, flat(ya), flat(yb), flat(yc), ug, *out_args,
                       rows_per_cond=batch * seq, cond0=0)

        up, q, k, v, ux, uy, ug = _in_proj(
            xs, mod, g1, w_in_b, l, rows_per_cond=dec_seq, cond0=1)
        d3 = lambda a: a.reshape(dec_batch, dec_seq, a.shape[-1])
        ya = _pool(d3(up), pool_w_b, pscale, l, nseq=1)
        yb = _attn_lat(d3(q), d3(k), d3(v), ck, cv, bias, l)
        yc, _ = _lru(d3(ux), d3(uy), state_lru, l, *lru_args)
        xs = _out_proj(xs, flat(ya), flat(yb), flat(yc), ug, *out_args,
                       rows_per_cond=dec_seq, cond0=1)

    y_prompt = xc.reshape(batch, seq, D_MODEL)
    y_sample = xs.reshape(dec_batch, dec_seq, D_MODEL)
    new_cache_k = new_k.reshape(batch, DEPTH, seq, N_HEADS, HEAD_DIM)
    new_cache_v = new_v.reshape(batch, DEPTH, seq, N_HEADS, HEAD_DIM)
    new_state_lru = jnp.stack(states, axis=1)
    return (y_prompt, y_sample, new_cache_k, new_cache_v, new_state_lru)
```

```python
import functools

import jax
import jax.numpy as jnp
import numpy as np
from jax import lax
from jax.experimental import pallas as pl
from jax.experimental.pallas import tpu as pltpu

D_MODEL = 1024
DEPTH = 4
GRID_W = 64
N_HEADS = 8
HEAD_DIM = 64
ATTN_W = N_HEADS * HEAD_DIM
NA_KH = 8
NA_KW = 16
POOL_WINDOWS = (2, 4, 8, 16)
N_POOL = 4
POOL_GRP = 128
POOL_W = N_POOL * POOL_GRP
LRU_W = 512
LRU_BLOCKS = 8
LRU_BLK = LRU_W // LRU_BLOCKS
LRU_C = 8.0
CONV_W = 4
N_BRANCH = 3
FF_HIDDEN = 2816
IN_W = POOL_W + 3 * ATTN_W + 2 * LRU_W + N_BRANCH * D_MODEL
EPS = 1e-6
NEG_INF = -1e30
ATTN_SCALE = HEAD_DIM ** -0.5

BF16 = jnp.bfloat16
F32 = jnp.float32

LANES = 128
SUBLANES = 8
MXU_DIM = 256
MIB = 1024 * 1024

ROW_TILE = 256
FF_CHUNKS = (1280, 1536)
MOD_COLS = 1536
POOL_PAD = 16
CONV_PAD = 8
SCAN_CHUNKS = SUBLANES
SCAN_UNROLL = 4
LAT_ROWS_PER_STEP = 4


def _params(semantics, vmem_mib):
    return pltpu.CompilerParams(dimension_semantics=semantics,
                                vmem_limit_bytes=vmem_mib * MIB)


def _resident(block_shape, index_map):
    return pl.BlockSpec(block_shape, index_map, pipeline_mode=pl.Buffered(1))


def _mm(a, b):
    return jnp.dot(a, b, preferred_element_type=F32)


def _mm_nt(a, b):
    return lax.dot_general(a, b, (((1,), (1,)), ((), ())), preferred_element_type=F32)


def _rmsnorm(x, g):
    ms = jnp.mean(x * x, axis=-1, keepdims=True)
    return (x * lax.rsqrt(ms + EPS)) * g


def _mod_kernel(c_ref, w_ref, b_ref, o_ref):
    c = c_ref[...]
    s = c * jax.nn.sigmoid(c)
    o_ref[...] = jnp.dot(s, w_ref[...], preferred_element_type=F32,
                         precision=lax.Precision.HIGHEST) + b_ref[...]


def _modulation(cond, w_mod, b_mod):
    n = 6 * D_MODEL
    return pl.pallas_call(
        _mod_kernel,
        grid=(DEPTH, n // MOD_COLS),
        in_specs=[
            pl.BlockSpec((SUBLANES, D_MODEL), lambda l, j: (0, 0)),
            pl.BlockSpec((None, D_MODEL, MOD_COLS), lambda l, j: (l, 0, j)),
            pl.BlockSpec((None, 1, MOD_COLS), lambda l, j: (l, 0, j)),
        ],
        out_specs=pl.BlockSpec((None, SUBLANES, MOD_COLS), lambda l, j: (l, 0, j)),
        out_shape=jax.ShapeDtypeStruct((DEPTH, SUBLANES, n), F32),
        compiler_params=_params(("arbitrary", "arbitrary"), 32),
        name="modulation",
    )(cond, w_mod, b_mod.reshape(DEPTH, 1, n))


def _in_proj_kernel(*refs, lru_seq, n_alias):
    x_ref, mod_ref, g_ref, w_ref = refs[:4]
    pos = 4
    if lru_seq:
        lru_in = refs[pos:pos + 7]
        pos += 7
    pos += n_alias
    up_ref, q_ref, k_ref, v_ref, ux_ref, uy_ref, ug_ref = refs[pos:pos + 7]
    scratch = refs[pos + 7:]
    x = x_ref[...]
    sh1 = mod_ref[:, 0:D_MODEL]
    sc1 = mod_ref[:, D_MODEL:2 * D_MODEL]
    h = (_rmsnorm(x, g_ref[...]) * (1.0 + sc1) + sh1).astype(BF16)

    def seg(lo, width):
        return _mm(h, w_ref[:, lo:lo + width])

    def up():
        up_ref[...] = seg(0, POOL_W)

    def q():
        q_ref[...] = (seg(POOL_W, ATTN_W) * ATTN_SCALE).astype(q_ref.dtype)

    def k():
        k_ref[...] = seg(POOL_W + ATTN_W, ATTN_W).astype(k_ref.dtype)

    def v():
        v_ref[...] = seg(POOL_W + 2 * ATTN_W, ATTN_W).astype(v_ref.dtype)

    off_x = POOL_W + 3 * ATTN_W
    off_g = off_x + 2 * LRU_W

    def ug(lo, width):
        def piece():
            ug_ref[:, lo:lo + width] = seg(off_g + lo, width)
        return piece

    if not lru_seq:
        up(), q(), k(), v()
        ux_ref[...] = seg(off_x, LRU_W)
        uy_ref[...] = seg(off_x + LRU_W, LRU_W)
        for n in range(N_BRANCH):
            ug(n * D_MODEL, D_MODEL)()
        return

    pieces = [up, q, k, v] + [ug(lo, LRU_W) for lo in range(0, N_BRANCH * D_MODEL, LRU_W)]

    def fill(n):
        for _ in range(min(n, len(pieces))):
            pieces.pop(0)()

    ux_s, uy_s, xc_s, a_s, b_s = scratch
    ux_s[...] = seg(off_x, LRU_W)
    uy_s[...] = seg(off_x + LRU_W, LRU_W)
    _lru_body(ux_s, uy_s, *lru_in, ux_ref, uy_ref, xc_s, a_s, b_s, seq=lru_seq, static_scan=True,
              fill=fill)
    fill(len(pieces))


def _in_proj(x, mod, g1, w_in, layer, *, rows_per_cond, cond0, caches=None, lru=None):
    m = x.shape[0]
    steps = m // ROW_TILE
    tiles_per_cond = rows_per_cond // ROW_TILE
    row = lambda i: (i, 0)
    per_layer = lambda shape: _resident((None,) + shape, lambda i: (layer,) + (0,) * len(shape))
    in_specs = [
        pl.BlockSpec((ROW_TILE, D_MODEL), row),
        pl.BlockSpec((None, None, 1, 6 * D_MODEL),
                     lambda i: (layer, cond0 + i // tiles_per_cond, 0, 0)),
        per_layer((1, D_MODEL)),
        per_layer((D_MODEL, IN_W)),
    ]
    args = [x, mod, g1, w_in]
    scratch = []
    if lru is not None:
        in_specs += [pl.BlockSpec((None, None, 2, LRU_W), lambda i: (i, 0, 0, 0))] + _lru_weight_specs(per_layer)
        args += list(lru)
        scratch = _lru_scratch(ROW_TILE)
    n_in = len(args)
    half = lambda dt: (pl.BlockSpec((ROW_TILE, ATTN_W), row), jax.ShapeDtypeStruct((m, ATTN_W), dt))
    aliases = {}
    if caches is None:
        kv = [half(BF16), half(BF16)]
    else:
        batch = caches[0]
        cache_spec = pl.BlockSpec((None, None, ROW_TILE, ATTN_W), lambda i: (i, layer, 0, 0))
        cache_shape = jax.ShapeDtypeStruct((batch, DEPTH, ROW_TILE, ATTN_W), F32)
        kv = [(cache_spec, cache_shape), (cache_spec, cache_shape)]
        if caches[1] is not None:
            in_specs += [pl.BlockSpec(memory_space=pl.ANY)] * 2
            args += [caches[1], caches[2]]
            aliases = {n_in: 2, n_in + 1: 3}
    if lru is None:
        branch_c = [half(F32), half(F32)]
    else:
        branch_c = [half(BF16), (pl.BlockSpec((None, 2, LRU_W), lambda i: (i, 0, 0)),
                                 jax.ShapeDtypeStruct((steps, 2, LRU_W), F32))]
    outs = [half(F32), half(BF16)] + kv + branch_c + [
        (pl.BlockSpec((ROW_TILE, N_BRANCH * D_MODEL), row),
         jax.ShapeDtypeStruct((m, N_BRANCH * D_MODEL), F32)),
    ]
    return pl.pallas_call(
        functools.partial(_in_proj_kernel, lru_seq=ROW_TILE if lru is not None else 0,
                          n_alias=len(aliases)),
        grid=(steps,),
        in_specs=in_specs,
        out_specs=[o[0] for o in outs],
        out_shape=[o[1] for o in outs],
        scratch_shapes=scratch,
        input_output_aliases=aliases,
        compiler_params=_params(("arbitrary",), 48),
        name="in_proj",
    )(*args)


def _pool_kernel(u_ref, w_ref, s_ref, o_ref, *, seq, nseq):
    t = lax.broadcasted_iota(jnp.int32, (seq, POOL_GRP), 0)
    zeros = jnp.zeros((POOL_PAD, POOL_GRP), F32)
    n = seq + 2 * POOL_PAD
    for b in range(nseq):
        for g, win in enumerate(POOL_WINDOWS):
            half = win // 2
            cols = slice(g * POOL_GRP, (g + 1) * POOL_GRP)
            x = u_ref[b, :, cols]
            xe = jnp.concatenate([zeros, x, zeros], axis=0)
            s = pltpu.roll(xe, 1, 0) + xe
            span = 1
            while span < half:
                s = pltpu.roll(s, span, 0) + pltpu.roll(s, n - span, 0)
                span *= 2
            s = s[POOL_PAD:POOL_PAD + seq]
            cnt = (jnp.minimum(t + half, seq) - jnp.maximum(t - half, 0)).astype(F32)
            d = s / cnt - x
            y = _mm(d.astype(BF16), w_ref[g])
            o_ref[b, :, cols] = (y * s_ref[:, cols]).astype(o_ref.dtype)


def _pool(u_pool, pool_w, pool_scale, layer, *, nseq):
    b, seq, _ = u_pool.shape
    blk = pl.BlockSpec((nseq, seq, POOL_W), lambda i: (i, 0, 0))
    return pl.pallas_call(
        functools.partial(_pool_kernel, seq=seq, nseq=nseq),
        grid=(b // nseq,),
        in_specs=[
            blk,
            _resident((None, N_POOL, POOL_GRP, POOL_GRP), lambda i: (layer, 0, 0, 0)),
            _resident((None, 1, POOL_W), lambda i: (layer, 0, 0)),
        ],
        out_specs=blk,
        out_shape=jax.ShapeDtypeStruct(u_pool.shape, BF16),
        compiler_params=_params(("arbitrary",), 48),
        name="pool",
    )(u_pool, pool_w, pool_scale)


def _first_head_lanes(rows):
    return lax.broadcasted_iota(jnp.int32, (rows, LANES), 1) < HEAD_DIM


def _attn_ctx_kernel(q_ref, k_ref, v_ref, o_ref, *, nseq):
    seq = q_ref.shape[1]
    first = _first_head_lanes(seq)
    ones = jnp.ones((seq, LANES), BF16)
    outs = []
    for b in range(nseq):
        pairs = []
        for p in range(ATTN_W // LANES):
            cols = slice(p * LANES, (p + 1) * LANES)
            q = q_ref[b, :, cols]
            k = k_ref[b, :, cols].astype(BF16)
            v_aug = jnp.concatenate([v_ref[b, :, cols].astype(BF16), ones], axis=1)
            halves = []
            for keep in (first, jnp.logical_not(first)):
                s = _mm_nt(jnp.where(keep, q, jnp.zeros_like(q)), k)
                e = jnp.exp(s - jnp.max(s, axis=-1, keepdims=True))
                ov = _mm(e.astype(BF16), v_aug)
                halves.append(ov[:, :LANES] / ov[:, LANES:])
            pairs.append(jnp.where(first, halves[0], halves[1]))
        outs.append(jnp.concatenate(pairs, axis=1).astype(o_ref.dtype))
    for b in range(nseq):
        o_ref[b] = outs[b]


def _attn_ctx(q, cache_k, cache_v, layer, *, nseq):
    b, seq, _ = q.shape
    blk = pl.BlockSpec((nseq, seq, ATTN_W), lambda i: (i, 0, 0))
    cache_blk = pl.BlockSpec((nseq, None, seq, ATTN_W), lambda i: (i, layer, 0, 0))
    return pl.pallas_call(
        functools.partial(_attn_ctx_kernel, nseq=nseq),
        grid=(b // nseq,),
        in_specs=[blk, cache_blk, cache_blk],
        out_specs=blk,
        out_shape=jax.ShapeDtypeStruct(q.shape, BF16),
        compiler_params=_params(("arbitrary",), 48),
        name="attn_ctx",
    )(q, cache_k, cache_v)


def _attn_lat_kernel(q_ref, k_ref, v_ref, ck_ref, cv_ref, bias_ref, o_ref, *, rows):
    j = pl.program_id(1)
    nsteps = pl.num_programs(1)
    span = NA_KH * GRID_W
    qcol = lax.broadcasted_iota(jnp.int32, (GRID_W, span), 0)
    kcol = lax.broadcasted_iota(jnp.int32, (GRID_W, span), 1) % GRID_W
    col0 = jnp.clip(qcol - NA_KW // 2, 0, GRID_W - NA_KW)
    col_ok = (kcol >= col0) & (kcol < col0 + NA_KW)
    edge = (j == 0) | (j == nsteps - 1)
    first = _first_head_lanes(GRID_W)
    ones_lat = jnp.ones((span, LANES), BF16)
    ones_ctx = jnp.ones((ck_ref.shape[0], LANES), BF16)

    def one_row(i, carry):
        r = j * LAT_ROWS_PER_STEP + i
        row0 = jnp.clip(r - NA_KH // 2, 0, rows - NA_KH)
        kstart = pl.multiple_of(row0 * GRID_W, GRID_W)
        qstart = pl.multiple_of(i * GRID_W, GRID_W)
        variant = jnp.where(edge, i, 0)
        pairs = []
        for p in range(ATTN_W // LANES):
            cols = slice(p * LANES, (p + 1) * LANES)
            q = q_ref[pl.ds(qstart, GRID_W), cols]
            kl = k_ref[pl.ds(kstart, span), cols]
            kc = ck_ref[:, cols]
            vl_aug = jnp.concatenate([v_ref[pl.ds(kstart, span), cols], ones_lat], axis=1)
            vc_aug = jnp.concatenate([cv_ref[:, cols], ones_ctx], axis=1)
            halves = []
            for half, keep in enumerate((first, jnp.logical_not(first))):
                qh = jnp.where(keep, q, jnp.zeros_like(q))
                s_lat = _mm_nt(qh, kl) + bias_ref[variant, 2 * p + half]
                s_lat = jnp.where(col_ok, s_lat, NEG_INF)
                s_ctx = _mm_nt(qh, kc)
                m = jnp.maximum(jnp.max(s_lat, axis=-1, keepdims=True),
                                jnp.max(s_ctx, axis=-1, keepdims=True))
                e_lat = jnp.exp(s_lat - m).astype(BF16)
                e_ctx = jnp.exp(s_ctx - m).astype(BF16)
                ov = _mm(e_lat, vl_aug) + _mm(e_ctx, vc_aug)
                halves.append(ov[:, :LANES] / ov[:, LANES:])
            pairs.append(jnp.where(first, halves[0], halves[1]))
        o_ref[pl.ds(qstart, GRID_W), :] = jnp.concatenate(pairs, axis=1).astype(o_ref.dtype)
        return carry

    lax.fori_loop(0, LAT_ROWS_PER_STEP, one_row, 0)


def _attn_lat(q, k, v, ck, cv, bias, layer):
    b, seq, _ = q.shape
    rows = seq // GRID_W
    steps = rows // LAT_ROWS_PER_STEP
    tq = LAT_ROWS_PER_STEP * GRID_W
    qblk = pl.BlockSpec((None, tq, ATTN_W), lambda bi, j: (bi, j, 0))
    full = pl.BlockSpec((None, seq, ATTN_W), lambda bi, j: (bi, 0, 0))
    past = ck.shape[2]
    ctx = pl.BlockSpec((None, None, past, ATTN_W), lambda bi, j: (bi, layer, 0, 0))
    bias_blk = pl.BlockSpec((None, LAT_ROWS_PER_STEP, N_HEADS, GRID_W, NA_KH * GRID_W),
                            lambda bi, j: (layer, jnp.minimum(j, 1), 0, 0, 0))
    return pl.pallas_call(
        functools.partial(_attn_lat_kernel, rows=rows),
        grid=(b, steps),
        in_specs=[qblk, full, full, ctx, ctx, bias_blk],
        out_specs=qblk,
        out_shape=jax.ShapeDtypeStruct(q.shape, BF16),
        compiler_params=_params(("arbitrary", "arbitrary"), 48),
        name="attn_lat",
    )(q, k, v, ck, cv, bias)


def _bias_table_kernel(ext_ref, o_ref):
    first = _first_head_lanes(GRID_W)
    for h in range(N_HEADS):
        lo, hi = [], []
        for d in range(2 * NA_KH - 1):
            x = jnp.broadcast_to(ext_ref[h, d:d + 1, :], (GRID_W, LANES))
            lo.append(pltpu.roll(x, GRID_W, 1, stride=1, stride_axis=0))
            hi.append(pltpu.roll(x, 0, 1, stride=1, stride_axis=0))
        for t in range(NA_KH):
            d0 = NA_KH - 1 - t
            groups = [jnp.where(first, lo[d0 + 2 * g], hi[d0 + 2 * g + 1]) for g in range(NA_KH // 2)]
            o_ref[t, h] = jnp.concatenate(groups, axis=1)


def _na_bias_variants(na_rpb):
    idx = np.clip(np.arange(LANES) - GRID_W, -(NA_KW - 1), NA_KW - 1) + (NA_KW - 1)
    ext = jnp.pad(na_rpb[:, :, :, idx], ((0, 0), (0, 0), (0, 1), (0, 0)))
    return pl.pallas_call(
        _bias_table_kernel,
        grid=(DEPTH,),
        in_specs=[pl.BlockSpec((None, N_HEADS, 2 * NA_KH, LANES), lambda l: (l, 0, 0, 0))],
        out_specs=pl.BlockSpec((None, NA_KH, N_HEADS, GRID_W, NA_KH * GRID_W),
                               lambda l: (l, 0, 0, 0, 0)),
        out_shape=jax.ShapeDtypeStruct((DEPTH, NA_KH, N_HEADS, GRID_W, NA_KH * GRID_W), F32),
        compiler_params=_params(("arbitrary",), 32),
        name="bias_table",
    )(ext)


def _log_sigmoid(x):
    return jnp.minimum(x, 0.0) - jnp.log1p(jnp.exp(-jnp.abs(x)))


def _loop(n, body, init, *, static, unroll=1):
    if static:
        carry = init
        for j in range(n):
            carry = body(j, carry)
        return carry
    return lax.fori_loop(0, n, body, init, unroll=unroll)


def _aligned(x, m):
    return x if isinstance(x, int) else pl.multiple_of(x, m)


def _lru_body(ux_ref, uy_ref, h0_ref, cw_ref, cb_ref, wg_ref, ba_ref, bx_ref, lam_ref,
              y_ref, st_ref, xc_s, a_s, b_s, *, seq, static_scan, fill=None):
    if fill is None:
        fill = lambda n: None
    chunk = seq // SCAN_CHUNKS
    pitch = chunk + SUBLANES
    nslab = LRU_W // LANES
    blk_rows = min(seq, ROW_TILE)
    nblk = seq // blk_rows

    zeros = jnp.zeros((CONV_PAD, LANES), F32)
    n = seq + 2 * CONV_PAD
    for s in range(nslab):
        cols = slice(s * LANES, (s + 1) * LANES)
        xe = jnp.concatenate([zeros, ux_ref[:, cols], zeros], axis=0)
        xc = cb_ref[:, cols] + cw_ref[0:1, cols] * pltpu.roll(xe, 2, 0)
        xc = xc + cw_ref[1:2, cols] * pltpu.roll(xe, 1, 0)
        xc = xc + cw_ref[2:3, cols] * xe
        xc = xc + cw_ref[3:4, cols] * pltpu.roll(xe, n - 1, 0)
        xc_s[:, cols] = xc[CONV_PAD:CONV_PAD + seq]

    decay = -LRU_C * _log_sigmoid(lam_ref[...])

    def gates(blk, carry):
        r0 = _aligned(blk * blk_rows, blk_rows)
        xc = xc_s[pl.ds(r0, blk_rows), :]
        xb = xc.astype(BF16)
        for e in range(2):
            def gate(kind, bias):
                lo = _mm(xb[:, :MXU_DIM], wg_ref[e * 4 + kind * 2])
                hi = _mm(xb[:, MXU_DIM:], wg_ref[e * 4 + kind * 2 + 1])
                return jax.nn.sigmoid(jnp.concatenate([lo, hi], axis=1) + bias)
            rg = gate(0, ba_ref[e:e + 1, :])
            ig = gate(1, bx_ref[e:e + 1, :])
            neg_log_a = rg * decay[e:e + 1, :]
            a = jnp.exp(-neg_log_a)
            z = jnp.tanh(neg_log_a) * (a * a + 1.0)
            root = jnp.where(z > 0.0, z * lax.rsqrt(z), 0.0)
            inp = root * ig * xc
            for s in range(nslab):
                cols = slice(s * LANES, (s + 1) * LANES)
                for c in range(blk_rows // chunk):
                    rows = slice(c * chunk, (c + 1) * chunk)
                    dst = pl.ds(_aligned((blk * (blk_rows // chunk) + c) * pitch, SUBLANES), chunk)
                    a_s[e, s, dst, :] = a[rows, cols]
                    b_s[e, s, dst, :] = inp[rows, cols]
            fill(2)
        return carry

    fill(1)
    _loop(nblk, gates, 0, static=nblk == 1)

    def strided(j):
        return pl.ds(j, SCAN_CHUNKS, stride=pitch)

    def position(e, j):
        return j if e == 0 else chunk - 1 - j

    def pass1(j, carry):
        out = []
        for e in range(2):
            idx = strided(position(e, j))
            for s in range(nslab):
                h, acc = carry[(e * nslab + s) * 2], carry[(e * nslab + s) * 2 + 1]
                a = a_s[e, s, idx, :]
                out += [a * h + b_s[e, s, idx, :], a * acc]
        return tuple(out)

    init = []
    for _ in range(2 * nslab):
        init += [jnp.zeros((SCAN_CHUNKS, LANES), F32), jnp.ones((SCAN_CHUNKS, LANES), F32)]
    agg = _loop(chunk, pass1, tuple(init), static=static_scan, unroll=SCAN_UNROLL)

    fill(1)
    starts = []
    for e in range(2):
        order = range(SCAN_CHUNKS) if e == 0 else range(SCAN_CHUNKS - 1, -1, -1)
        for s in range(nslab):
            cols = slice(s * LANES, (s + 1) * LANES)
            hend, atot = agg[(e * nslab + s) * 2], agg[(e * nslab + s) * 2 + 1]
            carry = h0_ref[e:e + 1, cols]
            rows = [None] * SCAN_CHUNKS
            for c in order:
                rows[c] = carry
                carry = atot[c:c + 1, :] * carry + hend[c:c + 1, :]
            st_ref[e:e + 1, cols] = carry
            starts.append(jnp.concatenate(rows, axis=0))

    def pass2(j, carry):
        out = []
        for e in range(2):
            idx = strided(position(e, j))
            for s in range(nslab):
                h = a_s[e, s, idx, :] * carry[e * nslab + s] + b_s[e, s, idx, :]
                b_s[e, s, idx, :] = h
                out.append(h)
        if static_scan and j == chunk // 2 - 1:
            fill(1)
        return tuple(out)

    _loop(chunk, pass2, tuple(starts), static=static_scan, unroll=SCAN_UNROLL)
    fill(1)

    def finish(blk, carry):
        for c in range(blk_rows // chunk):
            src = pl.ds(_aligned((blk * (blk_rows // chunk) + c) * pitch, SUBLANES), chunk)
            dst = pl.ds(_aligned(blk * blk_rows + c * chunk, chunk), chunk)
            for s in range(nslab):
                cols = slice(s * LANES, (s + 1) * LANES)
                hsum = b_s[0, s, src, :] + b_s[1, s, src, :]
                y_ref[dst, cols] = (hsum * jax.nn.gelu(uy_ref[dst, cols])).astype(y_ref.dtype)
        return carry

    _loop(nblk, finish, 0, static=nblk == 1)


def _lru_weight_specs(per_layer):
    return [per_layer((CONV_W, LRU_W)), per_layer((1, LRU_W)), per_layer((8, MXU_DIM, MXU_DIM)),
            per_layer((2, LRU_W)), per_layer((2, LRU_W)), per_layer((2, LRU_W))]


def _lru_scratch(seq):
    pitch = seq // SCAN_CHUNKS + SUBLANES
    scan_buf = pltpu.VMEM((2, LRU_W // LANES, SCAN_CHUNKS * pitch, LANES), F32)
    row_buf = pltpu.VMEM((seq, LRU_W), F32)
    return [row_buf, row_buf, row_buf, scan_buf, scan_buf]


def _lru(ux, uy, h0, h0_layer, conv_w, conv_b, wg, ba, bx, lam, layer):
    b, seq, _ = ux.shape
    blk = pl.BlockSpec((None, seq, LRU_W), lambda i: (i, 0, 0))
    per_layer = lambda shape: _resident((None,) + shape, lambda i: (layer,) + (0,) * len(shape))
    return pl.pallas_call(
        functools.partial(_lru_body, seq=seq, static_scan=False),
        grid=(b,),
        in_specs=[blk, blk, pl.BlockSpec((None, None, 2, LRU_W), lambda i: (i, h0_layer, 0, 0))]
        + _lru_weight_specs(per_layer),
        out_specs=[blk, pl.BlockSpec((None, 2, LRU_W), lambda i: (i, 0, 0))],
        out_shape=[jax.ShapeDtypeStruct(ux.shape, BF16), jax.ShapeDtypeStruct((b, 2, LRU_W), F32)],
        scratch_shapes=_lru_scratch(seq)[2:],
        compiler_params=_params(("arbitrary",), 56),
        name="rglru",
    )(ux, uy, h0, conv_w, conv_b, wg, ba, bx, lam)


def _gate_block_diag(w):
    per_tile = MXU_DIM // LRU_BLK
    w = w.reshape(DEPTH, 2, LRU_BLOCKS // per_tile, per_tile, LRU_BLK, LRU_BLK)
    eye = jnp.eye(per_tile, dtype=w.dtype)
    out = w[:, :, :, :, :, None, :] * eye[None, None, None, :, None, :, None]
    return out.reshape(DEPTH, 2, LRU_BLOCKS // per_tile, MXU_DIM, MXU_DIM)


def _out_proj_kernel(x_ref, ya_ref, yb_ref, yc_ref, ug_ref, mod_ref, g2_ref, wbr_ref, wout_ref,
                     wfi_ref, wfo_ref, gf_ref, o_ref, *, final):
    def mod(i):
        return mod_ref[:, i * D_MODEL:(i + 1) * D_MODEL]

    x = x_ref[...]
    merged = None
    for n, y_ref in enumerate((ya_ref, yb_ref, yc_ref)):
        proj = _mm(y_ref[...], wbr_ref[n])
        term = jax.nn.sigmoid(ug_ref[:, n * D_MODEL:(n + 1) * D_MODEL]) * proj
        merged = term if merged is None else merged + term
    x = x + mod(2) * _mm(merged.astype(BF16), wout_ref[...])

    h2 = (_rmsnorm(x, g2_ref[...]) * (1.0 + mod(4)) + mod(3)).astype(BF16)
    ffn = None
    lo = 0
    for width in FF_CHUNKS:
        fg = _mm(h2, wfi_ref[:, lo:lo + width])
        fu = _mm(h2, wfi_ref[:, FF_HIDDEN + lo:FF_HIDDEN + lo + width])
        act = ((fg * jax.nn.sigmoid(fg)) * fu).astype(BF16)
        part = _mm(act, wfo_ref[lo:lo + width, :])
        ffn = part if ffn is None else ffn + part
        lo += width
    x = x + mod(5) * ffn
    if final:
        x = _rmsnorm(x, gf_ref[...])
    o_ref[...] = x


def _out_proj(x, ya, yb, yc, ug, mod, g2, w_branch, w_out, w_ff_in, w_ff_out, g_final, layer,
              *, rows_per_cond, cond0):
    m = x.shape[0]
    tiles_per_cond = rows_per_cond // ROW_TILE
    row = lambda i: (i, 0)
    wide = pl.BlockSpec((ROW_TILE, D_MODEL), row)
    half = pl.BlockSpec((ROW_TILE, ATTN_W), row)
    per_layer = lambda shape: _resident((None,) + shape, lambda i: (layer,) + (0,) * len(shape))
    return pl.pallas_call(
        functools.partial(_out_proj_kernel, final=layer == DEPTH - 1),
        grid=(m // ROW_TILE,),
        in_specs=[
            wide, half, half, half,
            pl.BlockSpec((ROW_TILE, N_BRANCH * D_MODEL), row),
            pl.BlockSpec((None, None, 1, 6 * D_MODEL),
                         lambda i: (layer, cond0 + i // tiles_per_cond, 0, 0)),
            per_layer((1, D_MODEL)),
            per_layer((N_BRANCH, ATTN_W, D_MODEL)),
            per_layer((D_MODEL, D_MODEL)),
            per_layer((D_MODEL, 2 * FF_HIDDEN)),
            per_layer((FF_HIDDEN, D_MODEL)),
            _resident((1, D_MODEL), lambda i: (0, 0)),
        ],
        out_specs=wide,
        out_shape=jax.ShapeDtypeStruct((m, D_MODEL), F32),
        compiler_params=_params(("arbitrary",), 56),
        name="out_proj",
    )(x, ya, yb, yc, ug, mod, g2, w_branch, w_out, w_ff_in, w_ff_out, g_final)


def kernel(x_prompt, x_sample, cache_k, cache_v, state_lru, c, c_ctx, w_mod, b_mod, g_norm1, g_norm2, w_in, pool_w, pool_scale, na_rpb, lru_conv_w, lru_conv_b, lru_wa, lru_ba, lru_wx, lru_bx, lru_lambda, w_branch, w_out, w_ff_in, w_ff_out, g_final):
    batch, seq, _ = x_prompt.shape
    dec_batch, dec_seq, _ = x_sample.shape
    past = cache_k.shape[2]

    cond = jnp.concatenate([c_ctx[None, :], c,
                            jnp.zeros((SUBLANES - 1 - dec_batch, D_MODEL), F32)], axis=0)
    mod = _modulation(cond, w_mod, b_mod)[:, :1 + dec_batch].reshape(DEPTH, 1 + dec_batch, 1, 6 * D_MODEL)

    w_in_b = w_in.astype(BF16)
    pool_w_b = pool_w.astype(BF16)
    w_branch_b = w_branch.astype(BF16)
    w_out_b = w_out.astype(BF16)
    w_ff_in_b = w_ff_in.astype(BF16)
    w_ff_out_b = w_ff_out.astype(BF16)
    wg = jnp.stack([_gate_block_diag(lru_wa), _gate_block_diag(lru_wx)], axis=2)
    wg = wg.reshape(DEPTH, 8, MXU_DIM, MXU_DIM).astype(BF16)
    g1 = g_norm1.reshape(DEPTH, 1, D_MODEL)
    g2 = g_norm2.reshape(DEPTH, 1, D_MODEL)
    gf = g_final.reshape(1, D_MODEL)
    pscale = pool_scale.reshape(DEPTH, 1, POOL_W)
    conv_b = lru_conv_b.reshape(DEPTH, 1, LRU_W)
    ck = cache_k.reshape(dec_batch, DEPTH, past, ATTN_W).astype(BF16)
    cv = cache_v.reshape(dec_batch, DEPTH, past, ATTN_W).astype(BF16)
    bias = _na_bias_variants(na_rpb)
    h0_ctx = jnp.zeros((batch, 1, 2, LRU_W), F32)

    xc = x_prompt.reshape(batch * seq, D_MODEL)
    xs = x_sample.reshape(dec_batch * dec_seq, D_MODEL)
    new_k = new_v = None
    states = []
    for l in range(DEPTH):
        lru_args = (lru_conv_w, conv_b, wg, lru_ba, lru_bx, lru_lambda, l)
        out_args = (mod, g2, w_branch_b, w_out_b, w_ff_in_b, w_ff_out_b, gf, l)

        up, q, new_k, new_v, yc, st, ug = _in_proj(
            xc, mod, g1, w_in_b, l, rows_per_cond=batch * seq, cond0=0, caches=(batch, new_k, new_v),
            lru=(h0_ctx,) + lru_args[:-1])
        s3 = lambda a: a.reshape(batch, seq, a.shape[-1])
        ya = _pool(s3(up), pool_w_b, pscale, l, nseq=4)
        yb = _attn_ctx(s3(q), new_k, new_v, l, nseq=4)
        states.append(st)
        flat = lambda a: a.reshape(-1, a.shape[-1])
        xc = _out_proj(xc, flat(ya), flat(yb), yc, ug, *out_args,
                       rows_per_cond=batch * seq, cond0=0)

        up, q, k, v, ux, uy, ug = _in_proj(
            xs, mod, g1, w_in_b, l, rows_per_cond=dec_seq, cond0=1)
        d3 = lambda a: a.reshape(dec_batch, dec_seq, a.shape[-1])
        ya = _pool(d3(up), pool_w_b, pscale, l, nseq=1)
        yb = _attn_lat(d3(q), d3(k), d3(v), ck, cv, bias, l)
        yc, _ = _lru(d3(ux), d3(uy), state_lru, l, *lru_args)
        xs = _out_proj(xs, flat(ya), flat(yb), flat(yc), ug, *out_args,
                       rows_per_cond=dec_seq, cond0=1)

    y_prompt = xc.reshape(batch, seq, D_MODEL)
    y_sample = xs.reshape(dec_batch, dec_seq, D_MODEL)
    new_cache_k = new_k.reshape(batch, DEPTH, seq, N_HEADS, HEAD_DIM)
    new_cache_v = new_v.reshape(batch, DEPTH, seq, N_HEADS, HEAD_DIM)
    new_state_lru = jnp.stack(states, axis=1)
    return (y_prompt, y_sample, new_cache_k, new_cache_v, new_state_lru)
```

```python
import functools

import jax
import jax.numpy as jnp
import numpy as np
from jax import lax
from jax.experimental import pallas as pl
from jax.experimental.pallas import tpu as pltpu

D_MODEL = 1024
DEPTH = 4
GRID_W = 64
N_HEADS = 8
HEAD_DIM = 64
ATTN_W = N_HEADS * HEAD_DIM
NA_KH = 8
NA_KW = 16
POOL_WINDOWS = (2, 4, 8, 16)
N_POOL = 4
POOL_GRP = 128
POOL_W = N_POOL * POOL_GRP
LRU_W = 512
LRU_BLOCKS = 8
LRU_BLK = LRU_W // LRU_BLOCKS
LRU_C = 8.0
CONV_W = 4
N_BRANCH = 3
FF_HIDDEN = 2816
IN_W = POOL_W + 3 * ATTN_W + 2 * LRU_W + N_BRANCH * D_MODEL
EPS = 1e-6
NEG_INF = -1e30
ATTN_SCALE = HEAD_DIM ** -0.5

BF16 = jnp.bfloat16
F32 = jnp.float32
GATE_DTYPE = BF16

LANES = 128
SUBLANES = 8
MXU_DIM = 256
MIB = 1024 * 1024

ROW_TILE = 256
FF_CHUNKS = (1280, 1536)
MOD_COLS = 1536
POOL_PAD = 16
CONV_PAD = 8
SCAN_CHUNKS = SUBLANES
SCAN_UNROLL = 4
LAT_ROWS_PER_STEP = 4


def _params(semantics, vmem_mib):
    return pltpu.CompilerParams(dimension_semantics=semantics,
                                vmem_limit_bytes=vmem_mib * MIB)


def _resident(block_shape, index_map):
    return pl.BlockSpec(block_shape, index_map, pipeline_mode=pl.Buffered(1))


def _mm(a, b):
    return jnp.dot(a, b, preferred_element_type=F32)


def _mm_nt(a, b):
    return lax.dot_general(a, b, (((1,), (1,)), ((), ())), preferred_element_type=F32)


def _rmsnorm(x, g):
    ms = jnp.mean(x * x, axis=-1, keepdims=True)
    return (x * lax.rsqrt(ms + EPS)) * g


def _mod_kernel(c_ref, w_ref, b_ref, o_ref):
    c = c_ref[...]
    s = c * jax.nn.sigmoid(c)
    o_ref[...] = jnp.dot(s, w_ref[...], preferred_element_type=F32,
                         precision=lax.Precision.HIGHEST) + b_ref[...]


def _modulation(cond, w_mod, b_mod):
    n = 6 * D_MODEL
    return pl.pallas_call(
        _mod_kernel,
        grid=(DEPTH, n // MOD_COLS),
        in_specs=[
            pl.BlockSpec((SUBLANES, D_MODEL), lambda l, j: (0, 0)),
            pl.BlockSpec((None, D_MODEL, MOD_COLS), lambda l, j: (l, 0, j)),
            pl.BlockSpec((None, 1, MOD_COLS), lambda l, j: (l, 0, j)),
        ],
        out_specs=pl.BlockSpec((None, SUBLANES, MOD_COLS), lambda l, j: (l, 0, j)),
        out_shape=jax.ShapeDtypeStruct((DEPTH, SUBLANES, n), F32),
        compiler_params=_params(("arbitrary", "arbitrary"), 32),
        name="modulation",
    )(cond, w_mod, b_mod.reshape(DEPTH, 1, n))


def _in_proj_kernel(*refs, lru_seq, n_alias):
    x_ref, mod_ref, g_ref, w_ref = refs[:4]
    pos = 4
    if lru_seq:
        lru_in = refs[pos:pos + 7]
        pos += 7
    pos += n_alias
    up_ref, q_ref, k_ref, v_ref, ux_ref, uy_ref, ug_ref = refs[pos:pos + 7]
    scratch = refs[pos + 7:]
    x = x_ref[...]
    sh1 = mod_ref[:, 0:D_MODEL]
    sc1 = mod_ref[:, D_MODEL:2 * D_MODEL]
    h = (_rmsnorm(x, g_ref[...]) * (1.0 + sc1) + sh1).astype(BF16)

    def seg(lo, width):
        return _mm(h, w_ref[:, lo:lo + width])

    def up():
        up_ref[...] = seg(0, POOL_W)

    def q():
        q_ref[...] = (seg(POOL_W, ATTN_W) * ATTN_SCALE).astype(q_ref.dtype)

    def k():
        k_ref[...] = seg(POOL_W + ATTN_W, ATTN_W).astype(k_ref.dtype)

    def v():
        v_ref[...] = seg(POOL_W + 2 * ATTN_W, ATTN_W).astype(v_ref.dtype)

    off_x = POOL_W + 3 * ATTN_W
    off_g = off_x + 2 * LRU_W

    def ug(lo, width):
        def piece():
            ug_ref[:, lo:lo + width] = seg(off_g + lo, width).astype(ug_ref.dtype)
        return piece

    if not lru_seq:
        up(), q(), k(), v()
        ux_ref[...] = seg(off_x, LRU_W)
        uy_ref[...] = seg(off_x + LRU_W, LRU_W)
        for n in range(N_BRANCH):
            ug(n * D_MODEL, D_MODEL)()
        return

    pieces = [up, q, k, v] + [ug(lo, LRU_W) for lo in range(0, N_BRANCH * D_MODEL, LRU_W)]

    def fill(n):
        for _ in range(min(n, len(pieces))):
            pieces.pop(0)()

    ux_s, uy_s, xc_s, a_s, b_s = scratch
    ux_s[...] = seg(off_x, LRU_W)
    uy_s[...] = seg(off_x + LRU_W, LRU_W)
    _lru_body(ux_s, uy_s, *lru_in, ux_ref, uy_ref, xc_s, a_s, b_s, seq=lru_seq, static_scan=True,
              fill=fill)
    fill(len(pieces))


def _in_proj(x, mod, g1, w_in, layer, *, rows_per_cond, cond0, caches=None, lru=None):
    m = x.shape[0]
    steps = m // ROW_TILE
    tiles_per_cond = rows_per_cond // ROW_TILE
    row = lambda i: (i, 0)
    per_layer = lambda shape: _resident((None,) + shape, lambda i: (layer,) + (0,) * len(shape))
    in_specs = [
        pl.BlockSpec((ROW_TILE, D_MODEL), row),
        pl.BlockSpec((None, None, 1, 6 * D_MODEL),
                     lambda i: (layer, cond0 + i // tiles_per_cond, 0, 0)),
        per_layer((1, D_MODEL)),
        per_layer((D_MODEL, IN_W)),
    ]
    args = [x, mod, g1, w_in]
    scratch = []
    if lru is not None:
        in_specs += [pl.BlockSpec((None, None, 2, LRU_W), lambda i: (i, 0, 0, 0))] + _lru_weight_specs(per_layer)
        args += list(lru)
        scratch = _lru_scratch(ROW_TILE)
    n_in = len(args)
    half = lambda dt: (pl.BlockSpec((ROW_TILE, ATTN_W), row), jax.ShapeDtypeStruct((m, ATTN_W), dt))
    aliases = {}
    if caches is None:
        kv = [half(BF16), half(BF16)]
    else:
        batch = caches[0]
        cache_spec = pl.BlockSpec((None, None, ROW_TILE, ATTN_W), lambda i: (i, layer, 0, 0))
        cache_shape = jax.ShapeDtypeStruct((batch, DEPTH, ROW_TILE, ATTN_W), F32)
        kv = [(cache_spec, cache_shape), (cache_spec, cache_shape)]
        if caches[1] is not None:
            in_specs += [pl.BlockSpec(memory_space=pl.ANY)] * 2
            args += [caches[1], caches[2]]
            aliases = {n_in: 2, n_in + 1: 3}
    if lru is None:
        branch_c = [half(F32), half(F32)]
    else:
        branch_c = [half(BF16), (pl.BlockSpec((None, 2, LRU_W), lambda i: (i, 0, 0)),
                                 jax.ShapeDtypeStruct((steps, 2, LRU_W), F32))]
    outs = [half(F32), half(BF16)] + kv + branch_c + [
        (pl.BlockSpec((ROW_TILE, N_BRANCH * D_MODEL), row),
         jax.ShapeDtypeStruct((m, N_BRANCH * D_MODEL), GATE_DTYPE)),
    ]
    return pl.pallas_call(
        functools.partial(_in_proj_kernel, lru_seq=ROW_TILE if lru is not None else 0,
                          n_alias=len(aliases)),
        grid=(steps,),
        in_specs=in_specs,
        out_specs=[o[0] for o in outs],
        out_shape=[o[1] for o in outs],
        scratch_shapes=scratch,
        input_output_aliases=aliases,
        compiler_params=_params(("arbitrary",), 48),
        name="in_proj",
    )(*args)


def _pool_kernel(u_ref, w_ref, s_ref, o_ref, *, seq, nseq):
    t = lax.broadcasted_iota(jnp.int32, (seq, POOL_GRP), 0)
    zeros = jnp.zeros((POOL_PAD, POOL_GRP), F32)
    n = seq + 2 * POOL_PAD
    for b in range(nseq):
        for g, win in enumerate(POOL_WINDOWS):
            half = win // 2
            cols = slice(g * POOL_GRP, (g + 1) * POOL_GRP)
            x = u_ref[b, :, cols]
            xe = jnp.concatenate([zeros, x, zeros], axis=0)
            s = pltpu.roll(xe, 1, 0) + xe
            span = 1
            while span < half:
                s = pltpu.roll(s, span, 0) + pltpu.roll(s, n - span, 0)
                span *= 2
            s = s[POOL_PAD:POOL_PAD + seq]
            cnt = (jnp.minimum(t + half, seq) - jnp.maximum(t - half, 0)).astype(F32)
            d = s / cnt - x
            y = _mm(d.astype(BF16), w_ref[g])
            o_ref[b, :, cols] = (y * s_ref[:, cols]).astype(o_ref.dtype)


def _pool(u_pool, pool_w, pool_scale, layer, *, nseq):
    b, seq, _ = u_pool.shape
    blk = pl.BlockSpec((nseq, seq, POOL_W), lambda i: (i, 0, 0))
    return pl.pallas_call(
        functools.partial(_pool_kernel, seq=seq, nseq=nseq),
        grid=(b // nseq,),
        in_specs=[
            blk,
            _resident((None, N_POOL, POOL_GRP, POOL_GRP), lambda i: (layer, 0, 0, 0)),
            _resident((None, 1, POOL_W), lambda i: (layer, 0, 0)),
        ],
        out_specs=blk,
        out_shape=jax.ShapeDtypeStruct(u_pool.shape, BF16),
        compiler_params=_params(("arbitrary",), 48),
        name="pool",
    )(u_pool, pool_w, pool_scale)


def _first_head_lanes(rows):
    return lax.broadcasted_iota(jnp.int32, (rows, LANES), 1) < HEAD_DIM


def _attn_ctx_kernel(q_ref, k_ref, v_ref, o_ref, *, nseq):
    seq = q_ref.shape[1]
    first = _first_head_lanes(seq)
    ones = jnp.ones((seq, LANES), BF16)
    outs = []
    for b in range(nseq):
        pairs = []
        for p in range(ATTN_W // LANES):
            cols = slice(p * LANES, (p + 1) * LANES)
            q = q_ref[b, :, cols]
            k = k_ref[b, :, cols].astype(BF16)
            v_aug = jnp.concatenate([v_ref[b, :, cols].astype(BF16), ones], axis=1)
            halves = []
            for keep in (first, jnp.logical_not(first)):
                s = _mm_nt(jnp.where(keep, q, jnp.zeros_like(q)), k)
                e = jnp.exp(s - jnp.max(s, axis=-1, keepdims=True))
                ov = _mm(e.astype(BF16), v_aug)
                halves.append(ov[:, :LANES] / ov[:, LANES:])
            pairs.append(jnp.where(first, halves[0], halves[1]))
        outs.append(jnp.concatenate(pairs, axis=1).astype(o_ref.dtype))
    for b in range(nseq):
        o_ref[b] = outs[b]


def _attn_ctx(q, cache_k, cache_v, layer, *, nseq):
    b, seq, _ = q.shape
    blk = pl.BlockSpec((nseq, seq, ATTN_W), lambda i: (i, 0, 0))
    cache_blk = pl.BlockSpec((nseq, None, seq, ATTN_W), lambda i: (i, layer, 0, 0))
    return pl.pallas_call(
        functools.partial(_attn_ctx_kernel, nseq=nseq),
        grid=(b // nseq,),
        in_specs=[blk, cache_blk, cache_blk],
        out_specs=blk,
        out_shape=jax.ShapeDtypeStruct(q.shape, BF16),
        compiler_params=_params(("arbitrary",), 48),
        name="attn_ctx",
    )(q, cache_k, cache_v)


def _attn_lat_kernel(q_ref, k_ref, v_ref, ck_ref, cv_ref, bias_ref, o_ref, *, rows):
    j = pl.program_id(1)
    nsteps = pl.num_programs(1)
    span = NA_KH * GRID_W
    qcol = lax.broadcasted_iota(jnp.int32, (GRID_W, span), 0)
    kcol = lax.broadcasted_iota(jnp.int32, (GRID_W, span), 1) % GRID_W
    col0 = jnp.clip(qcol - NA_KW // 2, 0, GRID_W - NA_KW)
    col_ok = (kcol >= col0) & (kcol < col0 + NA_KW)
    edge = (j == 0) | (j == nsteps - 1)
    first = _first_head_lanes(GRID_W)
    ones_lat = jnp.ones((span, LANES), BF16)
    ones_ctx = jnp.ones((ck_ref.shape[0], LANES), BF16)

    def one_row(i, carry):
        r = j * LAT_ROWS_PER_STEP + i
        row0 = jnp.clip(r - NA_KH // 2, 0, rows - NA_KH)
        kstart = pl.multiple_of(row0 * GRID_W, GRID_W)
        qstart = pl.multiple_of(i * GRID_W, GRID_W)
        variant = jnp.where(edge, i, 0)
        pairs = []
        for p in range(ATTN_W // LANES):
            cols = slice(p * LANES, (p + 1) * LANES)
            q = q_ref[pl.ds(qstart, GRID_W), cols]
            kl = k_ref[pl.ds(kstart, span), cols]
            kc = ck_ref[:, cols]
            vl_aug = jnp.concatenate([v_ref[pl.ds(kstart, span), cols], ones_lat], axis=1)
            vc_aug = jnp.concatenate([cv_ref[:, cols], ones_ctx], axis=1)
            halves = []
            for half, keep in enumerate((first, jnp.logical_not(first))):
                qh = jnp.where(keep, q, jnp.zeros_like(q))
                s_lat = _mm_nt(qh, kl) + bias_ref[variant, 2 * p + half]
                s_lat = jnp.where(col_ok, s_lat, NEG_INF)
                s_ctx = _mm_nt(qh, kc)
                m = jnp.maximum(jnp.max(s_lat, axis=-1, keepdims=True),
                                jnp.max(s_ctx, axis=-1, keepdims=True))
                e_lat = jnp.exp(s_lat - m).astype(BF16)
                e_ctx = jnp.exp(s_ctx - m).astype(BF16)
                ov = _mm(e_lat, vl_aug) + _mm(e_ctx, vc_aug)
                halves.append(ov[:, :LANES] / ov[:, LANES:])
            pairs.append(jnp.where(first, halves[0], halves[1]))
        o_ref[pl.ds(qstart, GRID_W), :] = jnp.concatenate(pairs, axis=1).astype(o_ref.dtype)
        return carry

    lax.fori_loop(0, LAT_ROWS_PER_STEP, one_row, 0)


def _attn_lat(q, k, v, ck, cv, bias, layer):
    b, seq, _ = q.shape
    rows = seq // GRID_W
    steps = rows // LAT_ROWS_PER_STEP
    tq = LAT_ROWS_PER_STEP * GRID_W
    qblk = pl.BlockSpec((None, tq, ATTN_W), lambda bi, j: (bi, j, 0))
    full = pl.BlockSpec((None, seq, ATTN_W), lambda bi, j: (bi, 0, 0))
    past = ck.shape[2]
    ctx = pl.BlockSpec((None, None, past, ATTN_W), lambda bi, j: (bi, layer, 0, 0))
    bias_blk = pl.BlockSpec((None, LAT_ROWS_PER_STEP, N_HEADS, GRID_W, NA_KH * GRID_W),
                            lambda bi, j: (layer, jnp.minimum(j, 1), 0, 0, 0))
    return pl.pallas_call(
        functools.partial(_attn_lat_kernel, rows=rows),
        grid=(b, steps),
        in_specs=[qblk, full, full, ctx, ctx, bias_blk],
        out_specs=qblk,
        out_shape=jax.ShapeDtypeStruct(q.shape, BF16),
        compiler_params=_params(("arbitrary", "arbitrary"), 48),
        name="attn_lat",
    )(q, k, v, ck, cv, bias)


def _bias_table_kernel(ext_ref, o_ref):
    first = _first_head_lanes(GRID_W)
    for h in range(N_HEADS):
        lo, hi = [], []
        for d in range(2 * NA_KH - 1):
            x = jnp.broadcast_to(ext_ref[h, d:d + 1, :], (GRID_W, LANES))
            lo.append(pltpu.roll(x, GRID_W, 1, stride=1, stride_axis=0))
            hi.append(pltpu.roll(x, 0, 1, stride=1, stride_axis=0))
        for t in range(NA_KH):
            d0 = NA_KH - 1 - t
            groups = [jnp.where(first, lo[d0 + 2 * g], hi[d0 + 2 * g + 1]) for g in range(NA_KH // 2)]
            o_ref[t, h] = jnp.concatenate(groups, axis=1)


def _na_bias_variants(na_rpb):
    idx = np.clip(np.arange(LANES) - GRID_W, -(NA_KW - 1), NA_KW - 1) + (NA_KW - 1)
    ext = jnp.pad(na_rpb[:, :, :, idx], ((0, 0), (0, 0), (0, 1), (0, 0)))
    return pl.pallas_call(
        _bias_table_kernel,
        grid=(DEPTH,),
        in_specs=[pl.BlockSpec((None, N_HEADS, 2 * NA_KH, LANES), lambda l: (l, 0, 0, 0))],
        out_specs=pl.BlockSpec((None, NA_KH, N_HEADS, GRID_W, NA_KH * GRID_W),
                               lambda l: (l, 0, 0, 0, 0)),
        out_shape=jax.ShapeDtypeStruct((DEPTH, NA_KH, N_HEADS, GRID_W, NA_KH * GRID_W), F32),
        compiler_params=_params(("arbitrary",), 32),
        name="bias_table",
    )(ext)


def _log_sigmoid(x):
    return jnp.minimum(x, 0.0) - jnp.log1p(jnp.exp(-jnp.abs(x)))


def _loop(n, body, init, *, static, unroll=1):
    if static:
        carry = init
        for j in range(n):
            carry = body(j, carry)
        return carry
    return lax.fori_loop(0, n, body, init, unroll=unroll)


def _aligned(x, m):
    return x if isinstance(x, int) else pl.multiple_of(x, m)


def _lru_body(ux_ref, uy_ref, h0_ref, cw_ref, cb_ref, wg_ref, ba_ref, bx_ref, lam_ref,
              y_ref, st_ref, xc_s, a_s, b_s, *, seq, static_scan, fill=None):
    if fill is None:
        fill = lambda n: None
    chunk = seq // SCAN_CHUNKS
    pitch = chunk + SUBLANES
    nslab = LRU_W // LANES
    blk_rows = min(seq, ROW_TILE)
    nblk = seq // blk_rows

    zeros = jnp.zeros((CONV_PAD, LANES), F32)
    n = seq + 2 * CONV_PAD
    for s in range(nslab):
        cols = slice(s * LANES, (s + 1) * LANES)
        xe = jnp.concatenate([zeros, ux_ref[:, cols], zeros], axis=0)
        xc = cb_ref[:, cols] + cw_ref[0:1, cols] * pltpu.roll(xe, 2, 0)
        xc = xc + cw_ref[1:2, cols] * pltpu.roll(xe, 1, 0)
        xc = xc + cw_ref[2:3, cols] * xe
        xc = xc + cw_ref[3:4, cols] * pltpu.roll(xe, n - 1, 0)
        xc_s[:, cols] = xc[CONV_PAD:CONV_PAD + seq]

    decay = -LRU_C * _log_sigmoid(lam_ref[...])

    def gates(blk, carry):
        r0 = _aligned(blk * blk_rows, blk_rows)
        xc = xc_s[pl.ds(r0, blk_rows), :]
        xb = xc.astype(BF16)
        for e in range(2):
            def gate(kind, bias):
                lo = _mm(xb[:, :MXU_DIM], wg_ref[e * 4 + kind * 2])
                hi = _mm(xb[:, MXU_DIM:], wg_ref[e * 4 + kind * 2 + 1])
                return jax.nn.sigmoid(jnp.concatenate([lo, hi], axis=1) + bias)
            rg = gate(0, ba_ref[e:e + 1, :])
            ig = gate(1, bx_ref[e:e + 1, :])
            neg_log_a = rg * decay[e:e + 1, :]
            a = jnp.exp(-neg_log_a)
            z = jnp.tanh(neg_log_a) * (a * a + 1.0)
            root = jnp.where(z > 0.0, z * lax.rsqrt(z), 0.0)
            inp = root * ig * xc
            for s in range(nslab):
                cols = slice(s * LANES, (s + 1) * LANES)
                for c in range(blk_rows // chunk):
                    rows = slice(c * chunk, (c + 1) * chunk)
                    dst = pl.ds(_aligned((blk * (blk_rows // chunk) + c) * pitch, SUBLANES), chunk)
                    a_s[e, s, dst, :] = a[rows, cols]
                    b_s[e, s, dst, :] = inp[rows, cols]
            fill(2)
        return carry

    fill(1)
    _loop(nblk, gates, 0, static=nblk == 1)

    def strided(j):
        return pl.ds(j, SCAN_CHUNKS, stride=pitch)

    def position(e, j):
        return j if e == 0 else chunk - 1 - j

    def pass1(j, carry):
        out = []
        for e in range(2):
            idx = strided(position(e, j))
            for s in range(nslab):
                h, acc = carry[(e * nslab + s) * 2], carry[(e * nslab + s) * 2 + 1]
                a = a_s[e, s, idx, :]
                out += [a * h + b_s[e, s, idx, :], a * acc]
        return tuple(out)

    init = []
    for _ in range(2 * nslab):
        init += [jnp.zeros((SCAN_CHUNKS, LANES), F32), jnp.ones((SCAN_CHUNKS, LANES), F32)]
    agg = _loop(chunk, pass1, tuple(init), static=static_scan, unroll=SCAN_UNROLL)

    fill(1)
    starts = []
    for e in range(2):
        order = range(SCAN_CHUNKS) if e == 0 else range(SCAN_CHUNKS - 1, -1, -1)
        for s in range(nslab):
            cols = slice(s * LANES, (s + 1) * LANES)
            hend, atot = agg[(e * nslab + s) * 2], agg[(e * nslab + s) * 2 + 1]
            carry = h0_ref[e:e + 1, cols]
            rows = [None] * SCAN_CHUNKS
            for c in order:
                rows[c] = carry
                carry = atot[c:c + 1, :] * carry + hend[c:c + 1, :]
            st_ref[e:e + 1, cols] = carry
            starts.append(jnp.concatenate(rows, axis=0))

    def pass2(j, carry):
        out = []
        for e in range(2):
            idx = strided(position(e, j))
            for s in range(nslab):
                h = a_s[e, s, idx, :] * carry[e * nslab + s] + b_s[e, s, idx, :]
                b_s[e, s, idx, :] = h
                out.append(h)
        if static_scan and j == chunk // 2 - 1:
            fill(1)
        return tuple(out)

    _loop(chunk, pass2, tuple(starts), static=static_scan, unroll=SCAN_UNROLL)
    fill(1)

    def finish(blk, carry):
        for c in range(blk_rows // chunk):
            src = pl.ds(_aligned((blk * (blk_rows // chunk) + c) * pitch, SUBLANES), chunk)
            dst = pl.ds(_aligned(blk * blk_rows + c * chunk, chunk), chunk)
            for s in range(nslab):
                cols = slice(s * LANES, (s + 1) * LANES)
                hsum = b_s[0, s, src, :] + b_s[1, s, src, :]
                y_ref[dst, cols] = (hsum * jax.nn.gelu(uy_ref[dst, cols])).astype(y_ref.dtype)
        return carry

    _loop(nblk, finish, 0, static=nblk == 1)


def _lru_weight_specs(per_layer):
    return [per_layer((CONV_W, LRU_W)), per_layer((1, LRU_W)), per_layer((8, MXU_DIM, MXU_DIM)),
            per_layer((2, LRU_W)), per_layer((2, LRU_W)), per_layer((2, LRU_W))]


def _lru_scratch(seq):
    pitch = seq // SCAN_CHUNKS + SUBLANES
    scan_buf = pltpu.VMEM((2, LRU_W // LANES, SCAN_CHUNKS * pitch, LANES), F32)
    row_buf = pltpu.VMEM((seq, LRU_W), F32)
    return [row_buf, row_buf, row_buf, scan_buf, scan_buf]


def _lru(ux, uy, h0, h0_layer, conv_w, conv_b, wg, ba, bx, lam, layer):
    b, seq, _ = ux.shape
    blk = pl.BlockSpec((None, seq, LRU_W), lambda i: (i, 0, 0))
    per_layer = lambda shape: _resident((None,) + shape, lambda i: (layer,) + (0,) * len(shape))
    return pl.pallas_call(
        functools.partial(_lru_body, seq=seq, static_scan=False),
        grid=(b,),
        in_specs=[blk, blk, pl.BlockSpec((None, None, 2, LRU_W), lambda i: (i, h0_layer, 0, 0))]
        + _lru_weight_specs(per_layer),
        out_specs=[blk, pl.BlockSpec((None, 2, LRU_W), lambda i: (i, 0, 0))],
        out_shape=[jax.ShapeDtypeStruct(ux.shape, BF16), jax.ShapeDtypeStruct((b, 2, LRU_W), F32)],
        scratch_shapes=_lru_scratch(seq)[2:],
        compiler_params=_params(("arbitrary",), 56),
        name="rglru",
    )(ux, uy, h0, conv_w, conv_b, wg, ba, bx, lam)


def _gate_block_diag(w):
    per_tile = MXU_DIM // LRU_BLK
    w = w.reshape(DEPTH, 2, LRU_BLOCKS // per_tile, per_tile, LRU_BLK, LRU_BLK)
    eye = jnp.eye(per_tile, dtype=w.dtype)
    out = w[:, :, :, :, :, None, :] * eye[None, None, None, :, None, :, None]
    return out.reshape(DEPTH, 2, LRU_BLOCKS // per_tile, MXU_DIM, MXU_DIM)


def _out_proj_kernel(x_ref, ya_ref, yb_ref, yc_ref, ug_ref, mod_ref, g2_ref, wbr_ref, wout_ref,
                     wfi_ref, wfo_ref, gf_ref, o_ref, *, final):
    def mod(i):
        return mod_ref[:, i * D_MODEL:(i + 1) * D_MODEL]

    x = x_ref[...]
    merged = None
    for n, y_ref in enumerate((ya_ref, yb_ref, yc_ref)):
        proj = _mm(y_ref[...], wbr_ref[n])
        term = jax.nn.sigmoid(ug_ref[:, n * D_MODEL:(n + 1) * D_MODEL].astype(F32)) * proj
        merged = term if merged is None else merged + term
    x = x + mod(2) * _mm(merged.astype(BF16), wout_ref[...])

    h2 = (_rmsnorm(x, g2_ref[...]) * (1.0 + mod(4)) + mod(3)).astype(BF16)
    ffn = None
    lo = 0
    for width in FF_CHUNKS:
        fg = _mm(h2, wfi_ref[:, lo:lo + width])
        fu = _mm(h2, wfi_ref[:, FF_HIDDEN + lo:FF_HIDDEN + lo + width])
        act = ((fg * jax.nn.sigmoid(fg)) * fu).astype(BF16)
        part = _mm(act, wfo_ref[lo:lo + width, :])
        ffn = part if ffn is None else ffn + part
        lo += width
    x = x + mod(5) * ffn
    if final:
        x = _rmsnorm(x, gf_ref[...])
    o_ref[...] = x


def _out_proj(x, ya, yb, yc, ug, mod, g2, w_branch, w_out, w_ff_in, w_ff_out, g_final, layer,
              *, rows_per_cond, cond0):
    m = x.shape[0]
    tiles_per_cond = rows_per_cond // ROW_TILE
    row = lambda i: (i, 0)
    wide = pl.BlockSpec((ROW_TILE, D_MODEL), row)
    half = pl.BlockSpec((ROW_TILE, ATTN_W), row)
    per_layer = lambda shape: _resident((None,) + shape, lambda i: (layer,) + (0,) * len(shape))
    return pl.pallas_call(
        functools.partial(_out_proj_kernel, final=layer == DEPTH - 1),
        grid=(m // ROW_TILE,),
        in_specs=[
            wide, half, half, half,
            pl.BlockSpec((ROW_TILE, N_BRANCH * D_MODEL), row),
            pl.BlockSpec((None, None, 1, 6 * D_MODEL),
                         lambda i: (layer, cond0 + i // tiles_per_cond, 0, 0)),
            per_layer((1, D_MODEL)),
            per_layer((N_BRANCH, ATTN_W, D_MODEL)),
            per_layer((D_MODEL, D_MODEL)),
            per_layer((D_MODEL, 2 * FF_HIDDEN)),
            per_layer((FF_HIDDEN, D_MODEL)),
            _resident((1, D_MODEL), lambda i: (0, 0)),
        ],
        out_specs=wide,
        out_shape=jax.ShapeDtypeStruct((m, D_MODEL), F32),
        compiler_params=_params(("arbitrary",), 56),
        name="out_proj",
    )(x, ya, yb, yc, ug, mod, g2, w_branch, w_out, w_ff_in, w_ff_out, g_final)


def kernel(x_prompt, x_sample, cache_k, cache_v, state_lru, c, c_ctx, w_mod, b_mod, g_norm1, g_norm2, w_in, pool_w, pool_scale, na_rpb, lru_conv_w, lru_conv_b, lru_wa, lru_ba, lru_wx, lru_bx, lru_lambda, w_branch, w_out, w_ff_in, w_ff_out, g_final):
    batch, seq, _ = x_prompt.shape
    dec_batch, dec_seq, _ = x_sample.shape
    past = cache_k.shape[2]

    cond = jnp.concatenate([c_ctx[None, :], c,
                            jnp.zeros((SUBLANES - 1 - dec_batch, D_MODEL), F32)], axis=0)
    mod = _modulation(cond, w_mod, b_mod)[:, :1 + dec_batch].reshape(DEPTH, 1 + dec_batch, 1, 6 * D_MODEL)

    w_in_b = w_in.astype(BF16)
    pool_w_b = pool_w.astype(BF16)
    w_branch_b = w_branch.astype(BF16)
    w_out_b = w_out.astype(BF16)
    w_ff_in_b = w_ff_in.astype(BF16)
    w_ff_out_b = w_ff_out.astype(BF16)
    wg = jnp.stack([_gate_block_diag(lru_wa), _gate_block_diag(lru_wx)], axis=2)
    wg = wg.reshape(DEPTH, 8, MXU_DIM, MXU_DIM).astype(BF16)
    g1 = g_norm1.reshape(DEPTH, 1, D_MODEL)
    g2 = g_norm2.reshape(DEPTH, 1, D_MODEL)
    gf = g_final.reshape(1, D_MODEL)
    pscale = pool_scale.reshape(DEPTH, 1, POOL_W)
    conv_b = lru_conv_b.reshape(DEPTH, 1, LRU_W)
    ck = cache_k.reshape(dec_batch, DEPTH, past, ATTN_W).astype(BF16)
    cv = cache_v.reshape(dec_batch, DEPTH, past, ATTN_W).astype(BF16)
    bias = _na_bias_variants(na_rpb)
    h0_ctx = jnp.zeros((batch, 1, 2, LRU_W), F32)

    xc = x_prompt.reshape(batch * seq, D_MODEL)
    xs = x_sample.reshape(dec_batch * dec_seq, D_MODEL)
    new_k = new_v = None
    states = []
    for l in range(DEPTH):
        lru_args = (lru_conv_w, conv_b, wg, lru_ba, lru_bx, lru_lambda, l)
        out_args = (mod, g2, w_branch_b, w_out_b, w_ff_in_b, w_ff_out_b, gf, l)

        up, q, new_k, new_v, yc, st, ug = _in_proj(
            xc, mod, g1, w_in_b, l, rows_per_cond=batch * seq, cond0=0, caches=(batch, new_k, new_v),
            lru=(h0_ctx,) + lru_args[:-1])
        s3 = lambda a: a.reshape(batch, seq, a.shape[-1])
        ya = _pool(s3(up), pool_w_b, pscale, l, nseq=4)
        yb = _attn_ctx(s3(q), new_k, new_v, l, nseq=4)
        states.append(st)
        flat = lambda a: a.reshape(-1, a.shape[-1])
        xc = _out_proj(xc, flat(ya), flat(yb), yc, ug, *out_args,
                       rows_per_cond=batch * seq, cond0=0)

        up, q, k, v, ux, uy, ug = _in_proj(
            xs, mod, g1, w_in_b, l, rows_per_cond=dec_seq, cond0=1)
        d3 = lambda a: a.reshape(dec_batch, dec_seq, a.shape[-1])
        ya = _pool(d3(up), pool_w_b, pscale, l, nseq=1)
        yb = _attn_lat(d3(q), d3(k), d3(v), ck, cv, bias, l)
        yc, _ = _lru(d3(ux), d3(uy), state_lru, l, *lru_args)
        xs = _out_proj(xs, flat(ya), flat(yb), flat(yc), ug, *out_args,
                       rows_per_cond=dec_seq, cond0=1)

    y_prompt = xc.reshape(batch, seq, D_MODEL)
    y_sample = xs.reshape(dec_batch, dec_seq, D_MODEL)
    new_cache_k = new_k.reshape(batch, DEPTH, seq, N_HEADS, HEAD_DIM)
    new_cache_v = new_v.reshape(batch, DEPTH, seq, N_HEADS, HEAD_DIM)
    new_state_lru = jnp.stack(states, axis=1)
    return (y_prompt, y_sample, new_cache_k, new_cache_v, new_state_lru)
```

```python
import functools

import jax
import jax.numpy as jnp
import numpy as np
from jax import lax
from jax.experimental import pallas as pl
from jax.experimental.pallas import tpu as pltpu

D_MODEL = 1024
DEPTH = 4
GRID_W = 64
N_HEADS = 8
HEAD_DIM = 64
ATTN_W = N_HEADS * HEAD_DIM
NA_KH = 8
NA_KW = 16
POOL_WINDOWS = (2, 4, 8, 16)
N_POOL = 4
POOL_GRP = 128
POOL_W = N_POOL * POOL_GRP
LRU_W = 512
LRU_BLOCKS = 8
LRU_BLK = LRU_W // LRU_BLOCKS
LRU_C = 8.0
CONV_W = 4
N_BRANCH = 3
FF_HIDDEN = 2816
IN_W = POOL_W + 3 * ATTN_W + 2 * LRU_W + N_BRANCH * D_MODEL
EPS = 1e-6
NEG_INF = -1e30
ATTN_SCALE = HEAD_DIM ** -0.5

BF16 = jnp.bfloat16
F32 = jnp.float32

LANES = 128
SUBLANES = 8
MXU_DIM = 256
MIB = 1024 * 1024

ROW_TILE = 256
FF_CHUNKS = (1280, 1536)
MOD_COLS = 1536
POOL_PAD = 16
CONV_PAD = 8
SCAN_CHUNKS = SUBLANES
SCAN_UNROLL = 4
LAT_ROWS_PER_STEP = 4


def _params(semantics, vmem_mib):
    return pltpu.CompilerParams(dimension_semantics=semantics,
                                vmem_limit_bytes=vmem_mib * MIB)


def _resident(block_shape, index_map):
    return pl.BlockSpec(block_shape, index_map, pipeline_mode=pl.Buffered(1))


def _mm(a, b):
    return jnp.dot(a, b, preferred_element_type=F32)


def _mm_nt(a, b):
    return lax.dot_general(a, b, (((1,), (1,)), ((), ())), preferred_element_type=F32)


def _rmsnorm(x, g):
    ms = jnp.mean(x * x, axis=-1, keepdims=True)
    return (x * lax.rsqrt(ms + EPS)) * g


def _mod_kernel(c_ref, w_ref, b_ref, o_ref):
    c = c_ref[...]
    s = c * jax.nn.sigmoid(c)
    o_ref[...] = jnp.dot(s, w_ref[...], preferred_element_type=F32,
                         precision=lax.Precision.HIGHEST) + b_ref[...]


def _modulation(cond, w_mod, b_mod):
    n = 6 * D_MODEL
    return pl.pallas_call(
        _mod_kernel,
        grid=(DEPTH, n // MOD_COLS),
        in_specs=[
            pl.BlockSpec((SUBLANES, D_MODEL), lambda l, j: (0, 0)),
            pl.BlockSpec((None, D_MODEL, MOD_COLS), lambda l, j: (l, 0, j)),
            pl.BlockSpec((None, 1, MOD_COLS), lambda l, j: (l, 0, j)),
        ],
        out_specs=pl.BlockSpec((None, SUBLANES, MOD_COLS), lambda l, j: (l, 0, j)),
        out_shape=jax.ShapeDtypeStruct((DEPTH, SUBLANES, n), F32),
        compiler_params=_params(("arbitrary", "arbitrary"), 32),
        name="modulation",
    )(cond, w_mod, b_mod.reshape(DEPTH, 1, n))


def _in_proj_kernel(*refs, lru_seq, n_alias):
    x_ref, mod_ref, g_ref, w_ref = refs[:4]
    pos = 4
    if lru_seq:
        lru_in = refs[pos:pos + 7]
        pos += 7
    pos += n_alias
    up_ref, q_ref, k_ref, v_ref, ux_ref, uy_ref, ug_ref = refs[pos:pos + 7]
    scratch = refs[pos + 7:]
    x = x_ref[...]
    sh1 = mod_ref[:, 0:D_MODEL]
    sc1 = mod_ref[:, D_MODEL:2 * D_MODEL]
    h = (_rmsnorm(x, g_ref[...]) * (1.0 + sc1) + sh1).astype(BF16)

    def seg(lo, width):
        return _mm(h, w_ref[:, lo:lo + width])

    def up():
        up_ref[...] = seg(0, POOL_W)

    def q():
        q_ref[...] = (seg(POOL_W, ATTN_W) * ATTN_SCALE).astype(q_ref.dtype)

    def k():
        k_ref[...] = seg(POOL_W + ATTN_W, ATTN_W).astype(k_ref.dtype)

    def v():
        v_ref[...] = seg(POOL_W + 2 * ATTN_W, ATTN_W).astype(v_ref.dtype)

    off_x = POOL_W + 3 * ATTN_W
    off_g = off_x + 2 * LRU_W

    def ug(lo, width):
        def piece():
            ug_ref[:, lo:lo + width] = seg(off_g + lo, width)
        return piece

    if not lru_seq:
        up(), q(), k(), v()
        ux_ref[...] = seg(off_x, LRU_W)
        uy_ref[...] = seg(off_x + LRU_W, LRU_W)
        for n in range(N_BRANCH):
            ug(n * D_MODEL, D_MODEL)()
        return

    pieces = [up, q, k, v] + [ug(lo, LRU_W) for lo in range(0, N_BRANCH * D_MODEL, LRU_W)]

    def fill(n):
        for _ in range(min(n, len(pieces))):
            pieces.pop(0)()

    ux_s, uy_s, xc_s, a_s, b_s = scratch
    ux_s[...] = seg(off_x, LRU_W)
    uy_s[...] = seg(off_x + LRU_W, LRU_W)
    _lru_body(ux_s, uy_s, *lru_in, ux_ref, uy_ref, xc_s, a_s, b_s, seq=lru_seq, static_scan=True,
              fill=fill)
    fill(len(pieces))


def _in_proj(x, mod, g1, w_in, layer, *, rows_per_cond, cond0, caches=None, lru=None):
    m = x.shape[0]
    steps = m // ROW_TILE
    tiles_per_cond = rows_per_cond // ROW_TILE
    row = lambda i: (i, 0)
    per_layer = lambda shape: _resident((None,) + shape, lambda i: (layer,) + (0,) * len(shape))
    in_specs = [
        pl.BlockSpec((ROW_TILE, D_MODEL), row),
        pl.BlockSpec((None, None, 1, 6 * D_MODEL),
                     lambda i: (layer, cond0 + i // tiles_per_cond, 0, 0)),
        per_layer((1, D_MODEL)),
        per_layer((D_MODEL, IN_W)),
    ]
    args = [x, mod, g1, w_in]
    scratch = []
    if lru is not None:
        in_specs += [pl.BlockSpec((None, None, 2, LRU_W), lambda i: (i, 0, 0, 0))] + _lru_weight_specs(per_layer)
        args += list(lru)
        scratch = _lru_scratch(ROW_TILE)
    n_in = len(args)
    half = lambda dt: (pl.BlockSpec((ROW_TILE, ATTN_W), row), jax.ShapeDtypeStruct((m, ATTN_W), dt))
    aliases = {}
    if caches is None:
        kv = [half(BF16), half(BF16)]
    else:
        batch = caches[0]
        cache_spec = pl.BlockSpec((None, None, ROW_TILE, ATTN_W), lambda i: (i, layer, 0, 0))
        cache_shape = jax.ShapeDtypeStruct((batch, DEPTH, ROW_TILE, ATTN_W), F32)
        kv = [(cache_spec, cache_shape), (cache_spec, cache_shape)]
        if caches[1] is not None:
            in_specs += [pl.BlockSpec(memory_space=pl.ANY)] * 2
            args += [caches[1], caches[2]]
            aliases = {n_in: 2, n_in + 1: 3}
    if lru is None:
        branch_c = [half(F32), half(F32)]
    else:
        branch_c = [half(BF16), (pl.BlockSpec((None, 2, LRU_W), lambda i: (i, 0, 0)),
                                 jax.ShapeDtypeStruct((steps, 2, LRU_W), F32))]
    outs = [half(F32), half(BF16)] + kv + branch_c + [
        (pl.BlockSpec((ROW_TILE, N_BRANCH * D_MODEL), row),
         jax.ShapeDtypeStruct((m, N_BRANCH * D_MODEL), F32)),
    ]
    return pl.pallas_call(
        functools.partial(_in_proj_kernel, lru_seq=ROW_TILE if lru is not None else 0,
                          n_alias=len(aliases)),
        grid=(steps,),
        in_specs=in_specs,
        out_specs=[o[0] for o in outs],
        out_shape=[o[1] for o in outs],
        scratch_shapes=scratch,
        input_output_aliases=aliases,
        compiler_params=_params(("arbitrary",), 48),
        name="in_proj",
    )(*args)


def _pool_kernel(u_ref, w_ref, s_ref, o_ref, *, seq, nseq):
    t = lax.broadcasted_iota(jnp.int32, (seq, POOL_GRP), 0)
    zeros = jnp.zeros((POOL_PAD, POOL_GRP), F32)
    n = seq + 2 * POOL_PAD
    for b in range(nseq):
        for g, win in enumerate(POOL_WINDOWS):
            half = win // 2
            cols = slice(g * POOL_GRP, (g + 1) * POOL_GRP)
            x = u_ref[b, :, cols]
            xe = jnp.concatenate([zeros, x, zeros], axis=0)
            s = pltpu.roll(xe, 1, 0) + xe
            span = 1
            while span < half:
                s = pltpu.roll(s, span, 0) + pltpu.roll(s, n - span, 0)
                span *= 2
            s = s[POOL_PAD:POOL_PAD + seq]
            cnt = (jnp.minimum(t + half, seq) - jnp.maximum(t - half, 0)).astype(F32)
            d = s / cnt - x
            y = _mm(d.astype(BF16), w_ref[g])
            o_ref[b, :, cols] = (y * s_ref[:, cols]).astype(o_ref.dtype)


def _pool(u_pool, pool_w, pool_scale, layer, *, nseq):
    b, seq, _ = u_pool.shape
    blk = pl.BlockSpec((nseq, seq, POOL_W), lambda i: (i, 0, 0))
    return pl.pallas_call(
        functools.partial(_pool_kernel, seq=seq, nseq=nseq),
        grid=(b // nseq,),
        in_specs=[
            blk,
            _resident((None, N_POOL, POOL_GRP, POOL_GRP), lambda i: (layer, 0, 0, 0)),
            _resident((None, 1, POOL_W), lambda i: (layer, 0, 0)),
        ],
        out_specs=blk,
        out_shape=jax.ShapeDtypeStruct(u_pool.shape, BF16),
        compiler_params=_params(("arbitrary",), 48),
        name="pool",
    )(u_pool, pool_w, pool_scale)


def _first_head_lanes(rows):
    return lax.broadcasted_iota(jnp.int32, (rows, LANES), 1) < HEAD_DIM


def _attn_ctx_kernel(q_ref, k_ref, v_ref, o_ref, *, nseq):
    seq = q_ref.shape[1]
    first = _first_head_lanes(seq)
    ones = jnp.ones((seq, LANES), BF16)
    outs = []
    for b in range(nseq):
        pairs = []
        for p in range(ATTN_W // LANES):
            cols = slice(p * LANES, (p + 1) * LANES)
            q = q_ref[b, :, cols]
            k = k_ref[b, :, cols].astype(BF16)
            v_aug = jnp.concatenate([v_ref[b, :, cols].astype(BF16), ones], axis=1)
            halves = []
            for keep in (first, jnp.logical_not(first)):
                s = _mm_nt(jnp.where(keep, q, jnp.zeros_like(q)), k)
                e = jnp.exp(s - jnp.max(s, axis=-1, keepdims=True))
                ov = _mm(e.astype(BF16), v_aug)
                halves.append(ov[:, :LANES] / ov[:, LANES:])
            pairs.append(jnp.where(first, halves[0], halves[1]))
        outs.append(jnp.concatenate(pairs, axis=1).astype(o_ref.dtype))
    for b in range(nseq):
        o_ref[b] = outs[b]


def _attn_ctx(q, cache_k, cache_v, layer, *, nseq):
    b, seq, _ = q.shape
    blk = pl.BlockSpec((nseq, seq, ATTN_W), lambda i: (i, 0, 0))
    cache_blk = pl.BlockSpec((nseq, None, seq, ATTN_W), lambda i: (i, layer, 0, 0))
    return pl.pallas_call(
        functools.partial(_attn_ctx_kernel, nseq=nseq),
        grid=(b // nseq,),
        in_specs=[blk, cache_blk, cache_blk],
        out_specs=blk,
        out_shape=jax.ShapeDtypeStruct(q.shape, BF16),
        compiler_params=_params(("arbitrary",), 48),
        name="attn_ctx",
    )(q, cache_k, cache_v)


def _attn_lat_kernel(q_ref, k_ref, v_ref, ck_ref, cv_ref, bias_ref, o_ref, *, rows):
    j = pl.program_id(1)
    nsteps = pl.num_programs(1)
    span = NA_KH * GRID_W
    qcol = lax.broadcasted_iota(jnp.int32, (GRID_W, span), 0)
    kcol = lax.broadcasted_iota(jnp.int32, (GRID_W, span), 1) % GRID_W
    col0 = jnp.clip(qcol - NA_KW // 2, 0, GRID_W - NA_KW)
    col_ok = (kcol >= col0) & (kcol < col0 + NA_KW)
    edge = (j == 0) | (j == nsteps - 1)
    first = _first_head_lanes(GRID_W)
    ones_lat = jnp.ones((span, LANES), BF16)
    ones_ctx = jnp.ones((ck_ref.shape[0], LANES), BF16)

    def one_row(i, carry):
        r = j * LAT_ROWS_PER_STEP + i
        row0 = jnp.clip(r - NA_KH // 2, 0, rows - NA_KH)
        kstart = pl.multiple_of(row0 * GRID_W, GRID_W)
        qstart = pl.multiple_of(i * GRID_W, GRID_W)
        variant = jnp.where(edge, i, 0)
        pairs = []
        for p in range(ATTN_W // LANES):
            cols = slice(p * LANES, (p + 1) * LANES)
            q = q_ref[pl.ds(qstart, GRID_W), cols]
            kl = k_ref[pl.ds(kstart, span), cols]
            kc = ck_ref[:, cols]
            vl_aug = jnp.concatenate([v_ref[pl.ds(kstart, span), cols], ones_lat], axis=1)
            vc_aug = jnp.concatenate([cv_ref[:, cols], ones_ctx], axis=1)
            halves = []
            for half, keep in enumerate((first, jnp.logical_not(first))):
                qh = jnp.where(keep, q, jnp.zeros_like(q))
                s_lat = _mm_nt(qh, kl) + bias_ref[variant, 2 * p + half]
                s_lat = jnp.where(col_ok, s_lat, NEG_INF)
                s_ctx = _mm_nt(qh, kc)
                m = jnp.maximum(jnp.max(s_lat, axis=-1, keepdims=True),
                                jnp.max(s_ctx, axis=-1, keepdims=True))
                e_lat = jnp.exp(s_lat - m).astype(BF16)
                e_ctx = jnp.exp(s_ctx - m).astype(BF16)
                ov = _mm(e_lat, vl_aug) + _mm(e_ctx, vc_aug)
                halves.append(ov[:, :LANES] / ov[:, LANES:])
            pairs.append(jnp.where(first, halves[0], halves[1]))
        o_ref[pl.ds(qstart, GRID_W), :] = jnp.concatenate(pairs, axis=1).astype(o_ref.dtype)
        return carry

    lax.fori_loop(0, LAT_ROWS_PER_STEP, one_row, 0, unroll=True)


def _attn_lat(q, k, v, ck, cv, bias, layer):
    b, seq, _ = q.shape
    rows = seq // GRID_W
    steps = rows // LAT_ROWS_PER_STEP
    tq = LAT_ROWS_PER_STEP * GRID_W
    qblk = pl.BlockSpec((None, tq, ATTN_W), lambda bi, j: (bi, j, 0))
    full = pl.BlockSpec((None, seq, ATTN_W), lambda bi, j: (bi, 0, 0))
    past = ck.shape[2]
    ctx = pl.BlockSpec((None, None, past, ATTN_W), lambda bi, j: (bi, layer, 0, 0))
    bias_blk = pl.BlockSpec((None, LAT_ROWS_PER_STEP, N_HEADS, GRID_W, NA_KH * GRID_W),
                            lambda bi, j: (layer, jnp.minimum(j, 1), 0, 0, 0))
    return pl.pallas_call(
        functools.partial(_attn_lat_kernel, rows=rows),
        grid=(b, steps),
        in_specs=[qblk, full, full, ctx, ctx, bias_blk],
        out_specs=qblk,
        out_shape=jax.ShapeDtypeStruct(q.shape, BF16),
        compiler_params=_params(("arbitrary", "arbitrary"), 48),
        name="attn_lat",
    )(q, k, v, ck, cv, bias)


def _bias_table_kernel(ext_ref, o_ref):
    first = _first_head_lanes(GRID_W)
    for h in range(N_HEADS):
        lo, hi = [], []
        for d in range(2 * NA_KH - 1):
            x = jnp.broadcast_to(ext_ref[h, d:d + 1, :], (GRID_W, LANES))
            lo.append(pltpu.roll(x, GRID_W, 1, stride=1, stride_axis=0))
            hi.append(pltpu.roll(x, 0, 1, stride=1, stride_axis=0))
        for t in range(NA_KH):
            d0 = NA_KH - 1 - t
            groups = [jnp.where(first, lo[d0 + 2 * g], hi[d0 + 2 * g + 1]) for g in range(NA_KH // 2)]
            o_ref[t, h] = jnp.concatenate(groups, axis=1)


def _na_bias_variants(na_rpb):
    idx = np.clip(np.arange(LANES) - GRID_W, -(NA_KW - 1), NA_KW - 1) + (NA_KW - 1)
    ext = jnp.pad(na_rpb[:, :, :, idx], ((0, 0), (0, 0), (0, 1), (0, 0)))
    return pl.pallas_call(
        _bias_table_kernel,
        grid=(DEPTH,),
        in_specs=[pl.BlockSpec((None, N_HEADS, 2 * NA_KH, LANES), lambda l: (l, 0, 0, 0))],
        out_specs=pl.BlockSpec((None, NA_KH, N_HEADS, GRID_W, NA_KH * GRID_W),
                               lambda l: (l, 0, 0, 0, 0)),
        out_shape=jax.ShapeDtypeStruct((DEPTH, NA_KH, N_HEADS, GRID_W, NA_KH * GRID_W), F32),
        compiler_params=_params(("arbitrary",), 32),
        name="bias_table",
    )(ext)


def _log_sigmoid(x):
    return jnp.minimum(x, 0.0) - jnp.log1p(jnp.exp(-jnp.abs(x)))


def _loop(n, body, init, *, static, unroll=1):
    if static:
        carry = init
        for j in range(n):
            carry = body(j, carry)
        return carry
    return lax.fori_loop(0, n, body, init, unroll=unroll)


def _aligned(x, m):
    return x if isinstance(x, int) else pl.multiple_of(x, m)


def _lru_body(ux_ref, uy_ref, h0_ref, cw_ref, cb_ref, wg_ref, ba_ref, bx_ref, lam_ref,
              y_ref, st_ref, xc_s, a_s, b_s, *, seq, static_scan, fill=None):
    if fill is None:
        fill = lambda n: None
    chunk = seq // SCAN_CHUNKS
    pitch = chunk + SUBLANES
    nslab = LRU_W // LANES
    blk_rows = min(seq, ROW_TILE)
    nblk = seq // blk_rows

    zeros = jnp.zeros((CONV_PAD, LANES), F32)
    n = seq + 2 * CONV_PAD
    for s in range(nslab):
        cols = slice(s * LANES, (s + 1) * LANES)
        xe = jnp.concatenate([zeros, ux_ref[:, cols], zeros], axis=0)
        xc = cb_ref[:, cols] + cw_ref[0:1, cols] * pltpu.roll(xe, 2, 0)
        xc = xc + cw_ref[1:2, cols] * pltpu.roll(xe, 1, 0)
        xc = xc + cw_ref[2:3, cols] * xe
        xc = xc + cw_ref[3:4, cols] * pltpu.roll(xe, n - 1, 0)
        xc_s[:, cols] = xc[CONV_PAD:CONV_PAD + seq]

    decay = -LRU_C * _log_sigmoid(lam_ref[...])

    def gates(blk, carry):
        r0 = _aligned(blk * blk_rows, blk_rows)
        xc = xc_s[pl.ds(r0, blk_rows), :]
        xb = xc.astype(BF16)
        for e in range(2):
            def gate(kind, bias):
                lo = _mm(xb[:, :MXU_DIM], wg_ref[e * 4 + kind * 2])
                hi = _mm(xb[:, MXU_DIM:], wg_ref[e * 4 + kind * 2 + 1])
                return jax.nn.sigmoid(jnp.concatenate([lo, hi], axis=1) + bias)
            rg = gate(0, ba_ref[e:e + 1, :])
            ig = gate(1, bx_ref[e:e + 1, :])
            neg_log_a = rg * decay[e:e + 1, :]
            a = jnp.exp(-neg_log_a)
            z = jnp.tanh(neg_log_a) * (a * a + 1.0)
            root = jnp.where(z > 0.0, z * lax.rsqrt(z), 0.0)
            inp = root * ig * xc
            for s in range(nslab):
                cols = slice(s * LANES, (s + 1) * LANES)
                for c in range(blk_rows // chunk):
                    rows = slice(c * chunk, (c + 1) * chunk)
                    dst = pl.ds(_aligned((blk * (blk_rows // chunk) + c) * pitch, SUBLANES), chunk)
                    a_s[e, s, dst, :] = a[rows, cols]
                    b_s[e, s, dst, :] = inp[rows, cols]
            fill(2)
        return carry

    fill(1)
    _loop(nblk, gates, 0, static=nblk == 1)

    def strided(j):
        return pl.ds(j, SCAN_CHUNKS, stride=pitch)

    def position(e, j):
        return j if e == 0 else chunk - 1 - j

    def pass1(j, carry):
        out = []
        for e in range(2):
            idx = strided(position(e, j))
            for s in range(nslab):
                h, acc = carry[(e * nslab + s) * 2], carry[(e * nslab + s) * 2 + 1]
                a = a_s[e, s, idx, :]
                out += [a * h + b_s[e, s, idx, :], a * acc]
        return tuple(out)

    init = []
    for _ in range(2 * nslab):
        init += [jnp.zeros((SCAN_CHUNKS, LANES), F32), jnp.ones((SCAN_CHUNKS, LANES), F32)]
    agg = _loop(chunk, pass1, tuple(init), static=static_scan, unroll=SCAN_UNROLL)

    fill(1)
    starts = []
    for e in range(2):
        order = range(SCAN_CHUNKS) if e == 0 else range(SCAN_CHUNKS - 1, -1, -1)
        for s in range(nslab):
            cols = slice(s * LANES, (s + 1) * LANES)
            hend, atot = agg[(e * nslab + s) * 2], agg[(e * nslab + s) * 2 + 1]
            carry = h0_ref[e:e + 1, cols]
            rows = [None] * SCAN_CHUNKS
            for c in order:
                rows[c] = carry
                carry = atot[c:c + 1, :] * carry + hend[c:c + 1, :]
            st_ref[e:e + 1, cols] = carry
            starts.append(jnp.concatenate(rows, axis=0))

    def pass2(j, carry):
        out = []
        for e in range(2):
            idx = strided(position(e, j))
            for s in range(nslab):
                h = a_s[e, s, idx, :] * carry[e * nslab + s] + b_s[e, s, idx, :]
                b_s[e, s, idx, :] = h
                out.append(h)
        if static_scan and j == chunk // 2 - 1:
            fill(1)
        return tuple(out)

    _loop(chunk, pass2, tuple(starts), static=static_scan, unroll=SCAN_UNROLL)
    fill(1)

    def finish(blk, carry):
        for c in range(blk_rows // chunk):
            src = pl.ds(_aligned((blk * (blk_rows // chunk) + c) * pitch, SUBLANES), chunk)
            dst = pl.ds(_aligned(blk * blk_rows + c * chunk, chunk), chunk)
            for s in range(nslab):
                cols = slice(s * LANES, (s + 1) * LANES)
                hsum = b_s[0, s, src, :] + b_s[1, s, src, :]
                y_ref[dst, cols] = (hsum * jax.nn.gelu(uy_ref[dst, cols])).astype(y_ref.dtype)
        return carry

    _loop(nblk, finish, 0, static=nblk == 1)


def _lru_weight_specs(per_layer):
    return [per_layer((CONV_W, LRU_W)), per_layer((1, LRU_W)), per_layer((8, MXU_DIM, MXU_DIM)),
            per_layer((2, LRU_W)), per_layer((2, LRU_W)), per_layer((2, LRU_W))]


def _lru_scratch(seq):
    pitch = seq // SCAN_CHUNKS + SUBLANES
    scan_buf = pltpu.VMEM((2, LRU_W // LANES, SCAN_CHUNKS * pitch, LANES), F32)
    row_buf = pltpu.VMEM((seq, LRU_W), F32)
    return [row_buf, row_buf, row_buf, scan_buf, scan_buf]


def _lru(ux, uy, h0, h0_layer, conv_w, conv_b, wg, ba, bx, lam, layer):
    b, seq, _ = ux.shape
    blk = pl.BlockSpec((None, seq, LRU_W), lambda i: (i, 0, 0))
    per_layer = lambda shape: _resident((None,) + shape, lambda i: (layer,) + (0,) * len(shape))
    return pl.pallas_call(
        functools.partial(_lru_body, seq=seq, static_scan=False),
        grid=(b,),
        in_specs=[blk, blk, pl.BlockSpec((None, None, 2, LRU_W), lambda i: (i, h0_layer, 0, 0))]
        + _lru_weight_specs(per_layer),
        out_specs=[blk, pl.BlockSpec((None, 2, LRU_W), lambda i: (i, 0, 0))],
        out_shape=[jax.ShapeDtypeStruct(ux.shape, BF16), jax.ShapeDtypeStruct((b, 2, LRU_W), F32)],
        scratch_shapes=_lru_scratch(seq)[2:],
        compiler_params=_params(("arbitrary",), 56),
        name="rglru",
    )(ux, uy, h0, conv_w, conv_b, wg, ba, bx, lam)


def _gate_block_diag(w):
    per_tile = MXU_DIM // LRU_BLK
    w = w.reshape(DEPTH, 2, LRU_BLOCKS // per_tile, per_tile, LRU_BLK, LRU_BLK)
    eye = jnp.eye(per_tile, dtype=w.dtype)
    out = w[:, :, :, :, :, None, :] * eye[None, None, None, :, None, :, None]
    return out.reshape(DEPTH, 2, LRU_BLOCKS // per_tile, MXU_DIM, MXU_DIM)


def _out_proj_kernel(x_ref, ya_ref, yb_ref, yc_ref, ug_ref, mod_ref, g2_ref, wbr_ref, wout_ref,
                     wfi_ref, wfo_ref, gf_ref, o_ref, *, final):
    def mod(i):
        return mod_ref[:, i * D_MODEL:(i + 1) * D_MODEL]

    x = x_ref[...]
    merged = None
    for n, y_ref in enumerate((ya_ref, yb_ref, yc_ref)):
        proj = _mm(y_ref[...], wbr_ref[n])
        term = jax.nn.sigmoid(ug_ref[:, n * D_MODEL:(n + 1) * D_MODEL]) * proj
        merged = term if merged is None else merged + term
    x = x + mod(2) * _mm(merged.astype(BF16), wout_ref[...])

    h2 = (_rmsnorm(x, g2_ref[...]) * (1.0 + mod(4)) + mod(3)).astype(BF16)
    ffn = None
    lo = 0
    for width in FF_CHUNKS:
        fg = _mm(h2, wfi_ref[:, lo:lo + width])
        fu = _mm(h2, wfi_ref[:, FF_HIDDEN + lo:FF_HIDDEN + lo + width])
        act = ((fg * jax.nn.sigmoid(fg)) * fu).astype(BF16)
        part = _mm(act, wfo_ref[lo:lo + width, :])
        ffn = part if ffn is None else ffn + part
        lo += width
    x = x + mod(5) * ffn
    if final:
        x = _rmsnorm(x, gf_ref[...])
    o_ref[...] = x


def _out_proj(x, ya, yb, yc, ug, mod, g2, w_branch, w_out, w_ff_in, w_ff_out, g_final, layer,
              *, rows_per_cond, cond0):
    m = x.shape[0]
    tiles_per_cond = rows_per_cond // ROW_TILE
    row = lambda i: (i, 0)
    wide = pl.BlockSpec((ROW_TILE, D_MODEL), row)
    half = pl.BlockSpec((ROW_TILE, ATTN_W), row)
    per_layer = lambda shape: _resident((None,) + shape, lambda i: (layer,) + (0,) * len(shape))
    return pl.pallas_call(
        functools.partial(_out_proj_kernel, final=layer == DEPTH - 1),
        grid=(m // ROW_TILE,),
        in_specs=[
            wide, half, half, half,
            pl.BlockSpec((ROW_TILE, N_BRANCH * D_MODEL), row),
            pl.BlockSpec((None, None, 1, 6 * D_MODEL),
                         lambda i: (layer, cond0 + i // tiles_per_cond, 0, 0)),
            per_layer((1, D_MODEL)),
            per_layer((N_BRANCH, ATTN_W, D_MODEL)),
            per_layer((D_MODEL, D_MODEL)),
            per_layer((D_MODEL, 2 * FF_HIDDEN)),
            per_layer((FF_HIDDEN, D_MODEL)),
            _resident((1, D_MODEL), lambda i: (0, 0)),
        ],
        out_specs=wide,
        out_shape=jax.ShapeDtypeStruct((m, D_MODEL), F32),
        compiler_params=_params(("arbitrary",), 56),
        name="out_proj",
    )(x, ya, yb, yc, ug, mod, g2, w_branch, w_out, w_ff_in, w_ff_out, g_final)


def kernel(x_prompt, x_sample, cache_k, cache_v, state_lru, c, c_ctx, w_mod, b_mod, g_norm1, g_norm2, w_in, pool_w, pool_scale, na_rpb, lru_conv_w, lru_conv_b, lru_wa, lru_ba, lru_wx, lru_bx, lru_lambda, w_branch, w_out, w_ff_in, w_ff_out, g_final):
    batch, seq, _ = x_prompt.shape
    dec_batch, dec_seq, _ = x_sample.shape
    past = cache_k.shape[2]

    cond = jnp.concatenate([c_ctx[None, :], c,
                            jnp.zeros((SUBLANES - 1 - dec_batch, D_MODEL), F32)], axis=0)
    mod = _modulation(cond, w_mod, b_mod)[:, :1 + dec_batch].reshape(DEPTH, 1 + dec_batch, 1, 6 * D_MODEL)

    w_in_b = w_in.astype(BF16)
    pool_w_b = pool_w.astype(BF16)
    w_branch_b = w_branch.astype(BF16)
    w_out_b = w_out.astype(BF16)
    w_ff_in_b = w_ff_in.astype(BF16)
    w_ff_out_b = w_ff_out.astype(BF16)
    wg = jnp.stack([_gate_block_diag(lru_wa), _gate_block_diag(lru_wx)], axis=2)
    wg = wg.reshape(DEPTH, 8, MXU_DIM, MXU_DIM).astype(BF16)
    g1 = g_norm1.reshape(DEPTH, 1, D_MODEL)
    g2 = g_norm2.reshape(DEPTH, 1, D_MODEL)
    gf = g_final.reshape(1, D_MODEL)
    pscale = pool_scale.reshape(DEPTH, 1, POOL_W)
    conv_b = lru_conv_b.reshape(DEPTH, 1, LRU_W)
    ck = cache_k.reshape(dec_batch, DEPTH, past, ATTN_W).astype(BF16)
    cv = cache_v.reshape(dec_batch, DEPTH, past, ATTN_W).astype(BF16)
    bias = _na_bias_variants(na_rpb)
    h0_ctx = jnp.zeros((batch, 1, 2, LRU_W), F32)

    xc = x_prompt.reshape(batch * seq, D_MODEL)
    xs = x_sample.reshape(dec_batch * dec_seq, D_MODEL)
    new_k = new_v = None
    states = []
    for l in range(DEPTH):
        lru_args = (lru_conv_w, conv_b, wg, lru_ba, lru_bx, lru_lambda, l)
        out_args = (mod, g2, w_branch_b, w_out_b, w_ff_in_b, w_ff_out_b, gf, l)

        up, q, new_k, new_v, yc, st, ug = _in_proj(
            xc, mod, g1, w_in_b, l, rows_per_cond=batch * seq, cond0=0, caches=(batch, new_k, new_v),
            lru=(h0_ctx,) + lru_args[:-1])
        s3 = lambda a: a.reshape(batch, seq, a.shape[-1])
        ya = _pool(s3(up), pool_w_b, pscale, l, nseq=4)
        yb = _attn_ctx(s3(q), new_k, new_v, l, nseq=4)
        states.append(st)
        flat = lambda a: a.reshape(-1, a.shape[-1])
        xc = _out_proj(xc, flat(ya), flat(yb), yc, ug, *out_args,
                       rows_per_cond=batch * seq, cond0=0)

        up, q, k, v, ux, uy, ug = _in_proj(
            xs, mod, g1, w_in_b, l, rows_per_cond=dec_seq, cond0=1)
        d3 = lambda a: a.reshape(dec_batch, dec_seq, a.shape[-1])
        ya = _pool(d3(up), pool_w_b, pscale, l, nseq=1)
        yb = _attn_lat(d3(q), d3(k), d3(v), ck, cv, bias, l)
        yc, _ = _lru(d3(ux), d3(uy), state_lru, l, *lru_args)
        xs = _out_proj(xs, flat(ya), flat(yb), flat(yc), ug, *out_args,
                       rows_per_cond=dec_seq, cond0=1)

    y_prompt = xc.reshape(batch, seq, D_MODEL)
    y_sample = xs.reshape(dec_batch, dec_seq, D_MODEL)
    new_cache_k = new_k.reshape(batch, DEPTH, seq, N_HEADS, HEAD_DIM)
    new_cache_v = new_v.reshape(batch, DEPTH, seq, N_HEADS, HEAD_DIM)
    new_state_lru = jnp.stack(states, axis=1)
    return (y_prompt, y_sample, new_cache_k, new_cache_v, new_state_lru)
```
